```python
import math
import jax, jax.numpy as jnp
from jax import lax
import numpy as np

D_MODEL = 2048
BATCH = 16
SEQ = 256
DEPTH = 4
DEC_BATCH = 4
DEC_SEQ = 1024
PAST_LEN = 256

GRID_W = 64
MLA_HEADS = 8
MLA_NOPE = 128
MLA_ROPE = 64
MLA_V = 128
Q_LORA = 512
KV_LORA = 256
MLA_WIDTH = MLA_HEADS * MLA_V
FNET_GROUPS = 4
FNET_GROUP_DIM = 128
FNET_WIDTH = FNET_GROUPS * FNET_GROUP_DIM
RWKV_HEADS = 8
RWKV_HEAD_DIM = 64
RWKV_WIDTH = RWKV_HEADS * RWKV_HEAD_DIM
W_LORA = 64
A_LORA = 64
G_LORA = 128
MIX_WIDTH = MLA_WIDTH + FNET_WIDTH + RWKV_WIDTH
IN_SIZES = (Q_LORA, KV_LORA, MLA_ROPE, FNET_WIDTH, 3 * RWKV_WIDTH, W_LORA, A_LORA, G_LORA)
IN_WIDTH = Q_LORA + KV_LORA + MLA_ROPE + FNET_WIDTH + 3 * RWKV_WIDTH + W_LORA + A_LORA + G_LORA
D_FF = 5632
CONV_W = 3
QUERY_BLOCK = 128
ROPE_BASE = 10000.0
EPS = 1e-6
GN_EPS = 64e-5
DECAY_SCALE = math.exp(-0.5)

kernel_name = 'hybrid_mla_fnet_rwkv7_dit_step'


def _split(x, sizes):
    idx = np.cumsum(sizes)[:-1].tolist()
    return jnp.split(x, idx, axis=-1)


def rmsnorm(x, g):
    x32 = x.astype(jnp.float32)
    y = x32 * lax.rsqrt(jnp.mean(x32 * x32, axis=-1, keepdims=True) + EPS)
    return (y * g.astype(jnp.float32)).astype(x.dtype)


def dwconv3(x, w):
    xp = jnp.pad(x, ((0, 0), (1, 1), (0, 0)))
    return xp[:, :-2] * w[0] + xp[:, 1:-1] * w[1] + xp[:, 2:] * w[2]


def rope_tables(T, dtype):
    rows = T // GRID_W
    t = jnp.arange(rows * GRID_W)
    row = (t // GRID_W).astype(jnp.float32)
    col = (t % GRID_W).astype(jnp.float32)
    n = MLA_ROPE // 4
    inv = ROPE_BASE ** (-jnp.arange(n, dtype=jnp.float32) / n)
    ang = jnp.concatenate([row[:, None] * inv, col[:, None] * inv], axis=-1)
    return jnp.cos(ang).astype(dtype), jnp.sin(ang).astype(dtype)


def apply_rope(x, cos, sin):
    half = MLA_ROPE // 2
    x1, x2 = x[..., :half], x[..., half:]
    return jnp.concatenate([x1 * cos - x2 * sin, x1 * sin + x2 * cos], axis=-1)


def up_kv(c_kv, w_ukv):
    B, T, _ = c_kv.shape
    kv = (c_kv @ w_ukv).reshape(B, T, MLA_HEADS, MLA_NOPE + MLA_V)
    return kv[..., :MLA_NOPE], kv[..., MLA_NOPE:]


def block_attention(q_nope, q_rope, k_nope, k_rope, v):
    B, Tq, H, _ = q_nope.shape
    nb = Tq // QUERY_BLOCK
    scale = (MLA_NOPE + MLA_ROPE) ** -0.5

    def one_block(qs):
        qn, qr = qs
        s = jnp.einsum('bqhd,bkhd->bhqk', qn, k_nope) + jnp.einsum('bqhd,bkd->bhqk', qr, k_rope)
        p = jax.nn.softmax(s.astype(jnp.float32) * scale, axis=-1).astype(v.dtype)
        return jnp.einsum('bhqk,bkhd->bqhd', p, v)

    qn = q_nope.reshape(B, nb, QUERY_BLOCK, H, MLA_NOPE).swapaxes(0, 1)
    qr = q_rope.reshape(B, nb, QUERY_BLOCK, H, MLA_ROPE).swapaxes(0, 1)
    out = lax.map(one_block, (qn, qr))
    return out.swapaxes(0, 1).reshape(B, Tq, H * MLA_V)


def fourier_mix(xf):
    B, T, _ = xf.shape
    z = xf.astype(jnp.float32).reshape(B, T, FNET_GROUPS, FNET_GROUP_DIM)
    out = jnp.fft.fft2(z, axes=(1, 3), norm='ortho').real
    return out.reshape(B, T, FNET_WIDTH).astype(xf.dtype)


def rwkv_scan(r, w, kt, v, kh, a, s0):
    def step(S, inp):
        r_t, w_t, k_t, v_t, kh_t, a_t = inp
        sk = jnp.einsum('bhvk,bhk->bhv', S, kh_t)
        S = (S * w_t[:, :, None, :] - sk[..., None] * (a_t * kh_t)[:, :, None, :]
             + v_t[..., None] * k_t[:, :, None, :])
        return S, jnp.einsum('bhvk,bhk->bhv', S, r_t)

    xs = tuple(jnp.moveaxis(t, 1, 0) for t in (r, w, kt, v, kh, a))
    S, ys = lax.scan(step, s0, xs)
    return jnp.moveaxis(ys, 0, 1), S


def rwkv_mix(rkv, w_lo, a_lo, g_lo, p, s0_f, s0_b):
    B, T, _ = rkv.shape
    f32 = jnp.float32
    r, k, v = jnp.split(dwconv3(rkv, p['rwkv_conv']).astype(f32), 3, axis=-1)
    w_lo, a_lo, g_lo = w_lo.astype(f32), a_lo.astype(f32), g_lo.astype(f32)
    w = jnp.exp(-DECAY_SCALE * jax.nn.sigmoid(
        p['rwkv_w0'][:, None, None, :] + jnp.einsum('btr,drc->dbtc', jnp.tanh(w_lo), p['rwkv_w2'])))
    a = jax.nn.sigmoid(p['rwkv_a0'][:, None, None, :] + jnp.einsum('btr,drc->dbtc', a_lo, p['rwkv_a2']))
    g = jax.nn.sigmoid(g_lo) @ p['rwkv_g2']

    def heads(t):
        return t.reshape(t.shape[:-1] + (RWKV_HEADS, RWKV_HEAD_DIM))

    kappa = heads(k * p['rwkv_k_k'])
    kh = kappa * lax.rsqrt(jnp.sum(kappa * kappa, axis=-1, keepdims=True) + EPS)
    kt = heads(k * (1.0 + (a - 1.0) * p['rwkv_k_a']))
    rh, vh, wh, ah = heads(r), heads(v), heads(w), heads(a)
    y_f, S_f = rwkv_scan(rh, wh[0], kt[0], vh, kh, ah[0], s0_f.astype(f32))

    def flip(t):
        return t[:, ::-1]

    y_b, S_b = rwkv_scan(flip(rh), flip(wh[1]), flip(kt[1]), flip(vh), flip(kh), flip(ah[1]),
                         s0_b.astype(f32))
    y = y_f + flip(y_b)
    mu = jnp.mean(y, axis=-1, keepdims=True)
    var = jnp.mean((y - mu) ** 2, axis=-1, keepdims=True)
    yn = ((y - mu) * lax.rsqrt(var + GN_EPS)).reshape(B, T, RWKV_WIDTH) * p['rwkv_gn_g'] + p['rwkv_gn_b']
    bonus = jnp.sum(rh[None] * kt * heads(p['rwkv_r_k']), axis=-1, keepdims=True).sum(0) * vh
    out = (yn + bonus.reshape(B, T, RWKV_WIDTH)) * g
    return out.astype(rkv.dtype), S_f, S_b


def mixers(h, p, ctx, rope):
    B, T, _ = h.shape
    q_dn, kv_dn, k_rope, xf, rkv, w_lo, a_lo, g_lo = _split(h @ p['w_in'], IN_SIZES)
    q = (rmsnorm(q_dn, p['g_q_norm']) @ p['w_uq']).reshape(B, T, MLA_HEADS, MLA_NOPE + MLA_ROPE)
    q_nope, q_rope = q[..., :MLA_NOPE], q[..., MLA_NOPE:]
    c_kv = rmsnorm(kv_dn, p['g_kv_norm'])
    k_nope, v = up_kv(c_kv, p['w_ukv'])
    if ctx is None:
        k_nope_all, k_rope_all, v_all = k_nope, k_rope, v
        s0_f = jnp.zeros((B, RWKV_HEADS, RWKV_HEAD_DIM, RWKV_HEAD_DIM), jnp.float32)
        s0_b = s0_f
    else:
        ctx_ckv, ctx_krope, s0_f, s0_b = ctx
        cos, sin = rope
        q_rope = apply_rope(q_rope, cos[:, None], sin[:, None])
        k_rope_lat = apply_rope(k_rope, cos, sin)
        ck_nope, cv = up_kv(ctx_ckv.astype(h.dtype), p['w_ukv'])
        k_nope_all = jnp.concatenate([ck_nope, k_nope], axis=1)
        k_rope_all = jnp.concatenate([ctx_krope.astype(h.dtype), k_rope_lat], axis=1)
        v_all = jnp.concatenate([cv, v], axis=1)
    attn = block_attention(q_nope, q_rope, k_nope_all, k_rope_all, v_all)
    four = fourier_mix(xf)
    rw, S_f, S_b = rwkv_mix(rkv, w_lo, a_lo, g_lo, p, s0_f, s0_b)
    mix = jnp.concatenate([attn, four, rw], axis=-1)
    return mix, (c_kv, k_rope, S_f, S_b)


def conv_ffn(h, p):
    u = dwconv3(h @ p['ffn_w_up'], p['ffn_conv']) + p['ffn_conv_b']
    gate, val = jnp.split(u, 2, axis=-1)
    return (jax.nn.silu(gate) * val) @ p['ffn_w_down']


def trunk_layer(x, mod, p, ctx, rope):
    sh1, sc1, g1, sh2, sc2, g2 = jnp.split(mod[:, None, :].astype(x.dtype), 6, axis=-1)
    h = rmsnorm(x, p['g_pre_mix']) * (1.0 + sc1) + sh1
    mix, side = mixers(h, p, ctx, rope)
    x = x + g1 * rmsnorm(mix @ p['w_out'], p['g_post_mix'])
    h = rmsnorm(x, p['g_pre_ffn']) * (1.0 + sc2) + sh2
    x = x + g2 * rmsnorm(conv_ffn(h, p), p['g_post_ffn'])
    return x, side


def setup_inputs(seed: int = 0) -> dict:
    key = jax.random.key(seed)
    ks = jax.random.split(key, 40)

    def nrm(i, shape, scale):
        return jax.random.normal(ks[i], shape, jnp.float32) * scale

    L = DEPTH
    R = RWKV_WIDTH
    return {
        'x_prompt': nrm(0, (BATCH, SEQ, D_MODEL), 1.0),
        'x_sample': nrm(1, (DEC_BATCH, DEC_SEQ, D_MODEL), 1.0),
        'cache_mla_ckv': nrm(2, (DEC_BATCH, L, PAST_LEN, KV_LORA), 1.0),
        'cache_mla_krope': nrm(3, (DEC_BATCH, L, PAST_LEN, MLA_ROPE), 1.0),
        'state_rwkv': nrm(4, (DEC_BATCH, L, 2, RWKV_HEADS, RWKV_HEAD_DIM, RWKV_HEAD_DIM), 0.1),
        'c': nrm(5, (DEC_BATCH, D_MODEL), 1.0),
        'c_ctx': nrm(6, (D_MODEL,), 1.0),
        'w_mod': nrm(7, (L, D_MODEL, 6 * D_MODEL), 0.5 * D_MODEL ** -0.5),
        'b_mod': nrm(8, (L, 6 * D_MODEL), 0.01),
        'g_pre_mix': 1.0 + nrm(9, (L, D_MODEL), 0.05),
        'g_post_mix': 1.0 + nrm(10, (L, D_MODEL), 0.05),
        'g_pre_ffn': 1.0 + nrm(11, (L, D_MODEL), 0.05),
        'g_post_ffn': 1.0 + nrm(12, (L, D_MODEL), 0.05),
        'w_in': nrm(13, (L, D_MODEL, IN_WIDTH), D_MODEL ** -0.5),
        'g_q_norm': 1.0 + nrm(14, (L, Q_LORA), 0.05),
        'w_uq': nrm(15, (L, Q_LORA, MLA_HEADS * (MLA_NOPE + MLA_ROPE)), Q_LORA ** -0.5),
        'g_kv_norm': 1.0 + nrm(16, (L, KV_LORA), 0.05),
        'w_ukv': nrm(17, (L, KV_LORA, MLA_HEADS * (MLA_NOPE + MLA_V)), KV_LORA ** -0.5),
        'rwkv_conv': nrm(18, (L, CONV_W, 3 * R), CONV_W ** -0.5),
        'rwkv_w0': nrm(19, (L, 2, R), 1.0),
        'rwkv_w2': nrm(20, (L, 2, W_LORA, R), 0.5 * W_LORA ** -0.5),
        'rwkv_a0': nrm(21, (L, 2, R), 0.5),
        'rwkv_a2': nrm(22, (L, 2, A_LORA, R), 0.5 * A_LORA ** -0.5),
        'rwkv_g2': nrm(23, (L, G_LORA, R), G_LORA ** -0.5),
        'rwkv_k_k': 0.85 + nrm(24, (L, R), 0.05),
        'rwkv_k_a': 1.0 + nrm(25, (L, R), 0.05),
        'rwkv_r_k': nrm(26, (L, R), 0.1),
        'rwkv_gn_g': 1.0 + nrm(27, (L, R), 0.05),
        'rwkv_gn_b': nrm(28, (L, R), 0.01),
        'w_out': nrm(29, (L, MIX_WIDTH, D_MODEL), MIX_WIDTH ** -0.5),
        'ffn_w_up': nrm(30, (L, D_MODEL, 2 * D_FF), D_MODEL ** -0.5),
        'ffn_conv': nrm(31, (L, CONV_W, 2 * D_FF), CONV_W ** -0.5),
        'ffn_conv_b': nrm(32, (L, 2 * D_FF), 0.01),
        'ffn_w_down': nrm(33, (L, D_FF, D_MODEL), D_FF ** -0.5),
    }


def reference(x_prompt, x_sample, cache_mla_ckv, cache_mla_krope, state_rwkv, c, c_ctx,
              w_mod, b_mod, g_pre_mix, g_post_mix, g_pre_ffn, g_post_ffn,
              w_in, g_q_norm, w_uq, g_kv_norm, w_ukv,
              rwkv_conv, rwkv_w0, rwkv_w2, rwkv_a0, rwkv_a2, rwkv_g2,
              rwkv_k_k, rwkv_k_a, rwkv_r_k, rwkv_gn_g, rwkv_gn_b,
              w_out, ffn_w_up, ffn_conv, ffn_conv_b, ffn_w_down):
    rope = rope_tables(x_sample.shape[1], x_sample.dtype)
    xp, xs = x_prompt, x_sample
    ckv_out, krope_out, state_out = [], [], []
    for l in range(DEPTH):
        p = {
            'g_pre_mix': g_pre_mix[l], 'g_post_mix': g_post_mix[l],
            'g_pre_ffn': g_pre_ffn[l], 'g_post_ffn': g_post_ffn[l],
            'w_in': w_in[l], 'g_q_norm': g_q_norm[l], 'w_uq': w_uq[l],
            'g_kv_norm': g_kv_norm[l], 'w_ukv': w_ukv[l],
            'rwkv_conv': rwkv_conv[l], 'rwkv_w0': rwkv_w0[l], 'rwkv_w2': rwkv_w2[l],
            'rwkv_a0': rwkv_a0[l], 'rwkv_a2': rwkv_a2[l], 'rwkv_g2': rwkv_g2[l],
            'rwkv_k_k': rwkv_k_k[l], 'rwkv_k_a': rwkv_k_a[l], 'rwkv_r_k': rwkv_r_k[l],
            'rwkv_gn_g': rwkv_gn_g[l], 'rwkv_gn_b': rwkv_gn_b[l],
            'w_out': w_out[l], 'ffn_w_up': ffn_w_up[l], 'ffn_conv': ffn_conv[l],
            'ffn_conv_b': ffn_conv_b[l], 'ffn_w_down': ffn_w_down[l],
        }
        mod_ctx = jax.nn.silu(c_ctx)[None, :] @ w_mod[l] + b_mod[l]
        mod_lat = jax.nn.silu(c) @ w_mod[l] + b_mod[l]
        xp, (ckv, kr, s_f, s_b) = trunk_layer(xp, mod_ctx, p, None, None)
        ckv_out.append(ckv)
        krope_out.append(kr)
        state_out.append(jnp.stack([s_f, s_b], axis=1))
        ctx = (cache_mla_ckv[:, l], cache_mla_krope[:, l], state_rwkv[:, l, 0], state_rwkv[:, l, 1])
        xs, _ = trunk_layer(xs, mod_lat, p, ctx, rope)
    new_mla_ckv = jnp.stack(ckv_out, axis=1)
    new_mla_krope = jnp.stack(krope_out, axis=1)
    new_rwkv_state = jnp.stack(state_out, axis=1)
    return (xp, xs, new_mla_ckv, new_mla_krope, new_rwkv_state)
```

```python
import functools
import math

import jax
import jax.numpy as jnp
import numpy as np
from jax import lax
from jax.experimental import pallas as pl
from jax.experimental.pallas import tpu as pltpu

F32 = jnp.float32
BF16 = jnp.bfloat16

D_MODEL = 2048
SEQ = 256
DEC_SEQ = 1024
PAST_LEN = 256
GRID_W = 64
MLA_HEADS = 8
MLA_NOPE = 128
MLA_ROPE = 64
MLA_V = 128
Q_LORA = 512
KV_LORA = 256
FNET_GROUPS = 4
FNET_GROUP_DIM = 128
FNET_WIDTH = FNET_GROUPS * FNET_GROUP_DIM
RWKV_HEADS = 8
RWKV_HEAD_DIM = 64
RWKV_WIDTH = RWKV_HEADS * RWKV_HEAD_DIM
W_LORA = 64
A_LORA = 64
G_LORA = 128
D_FF = 5632
ROPE_BASE = 10000.0
EPS = 1e-6
GN_EPS = 64e-5
DECAY_SCALE = math.exp(-0.5)

LANE = 128
ROW_GROUP = 256
HEAD_PAD = 256
CHUNK = 64
VMEM_CAP = 56 * 1024 * 1024

COL_RKV = 0
COL_QDN = 3 * RWKV_WIDTH
COL_XF = COL_QDN + Q_LORA
COL_KVDN = COL_XF + FNET_WIDTH
COL_KROPE = COL_KVDN + KV_LORA
COL_WA = COL_KROPE + LANE
COL_GLO = COL_WA + LANE
IN_PAD = COL_GLO + G_LORA


SMALL_OPERANDS = 8 * 1024 * 1024


def _params(sem, nbytes):
    return pltpu.CompilerParams(dimension_semantics=sem,
                                vmem_limit_bytes=int(min(VMEM_CAP, nbytes + SMALL_OPERANDS)))


def _dot(a, b):
    return jnp.dot(a, b, preferred_element_type=F32)


def _dot_nt(a, b):
    return lax.dot_general(a, b, (((1,), (1,)), ((), ())), preferred_element_type=F32)


def _dot_tn(a, b):
    return lax.dot_general(a, b, (((0,), (0,)), ((), ())), preferred_element_type=F32)


def _split_bf16(x, n):
    terms = []
    rem = x
    for i in range(n):
        t = rem.astype(BF16)
        terms.append(t)
        if i + 1 < n:
            rem = rem - t.astype(F32)
    return terms


def _dot_exact_rhs(a, b_exact_bf16, n):
    acc = None
    for t in _split_bf16(a, n):
        p = _dot(t, b_exact_bf16)
        acc = p if acc is None else acc + p
    return acc


def _dot_x3(a, b, fn=_dot):
    ah, al = _split_bf16(a, 2)
    bh, bl = _split_bf16(b, 2)
    return fn(ah, bh) + (fn(ah, bl) + fn(al, bh))


def _rms(x, g):
    return x * lax.rsqrt(jnp.mean(x * x, axis=-1, keepdims=True) + EPS) * g


def _sigmoid(x):
    return 1.0 / (1.0 + jnp.exp(-x))


def _mod_kernel(cond_ref, w_ref, b_ref, o_ref):
    c = cond_ref[...]
    s = (c * _sigmoid(c)).astype(BF16)
    o_ref[0] = _dot(s, w_ref[0].astype(BF16)) + b_ref[0]


def _modulation(cond, w_mod, b_mod):
    L, D, N = w_mod.shape
    tn = 1024
    return pl.pallas_call(
        _mod_kernel,
        grid=(L, N // tn),
        in_specs=[pl.BlockSpec((8, D), lambda l, j: (0, 0)),
                  pl.BlockSpec((1, D, tn), lambda l, j: (l, 0, j)),
                  pl.BlockSpec((1, 1, tn), lambda l, j: (l, 0, j))],
        out_specs=pl.BlockSpec((1, 8, tn), lambda l, j: (l, 0, j)),
        out_shape=jax.ShapeDtypeStruct((L, 8, N), F32),
        compiler_params=_params(("parallel", "parallel"), 3 * D * tn * 4),
        name="modulation",
    )(cond, w_mod, b_mod.reshape(L, 1, N))


def _inproj_kernel(x_ref, g_ref, mod_ref, w_ref, o_ref, a_scr):
    @pl.when(pl.program_id(1) == 0)
    def _():
        h = _rms(x_ref[...], g_ref[...]) * (1.0 + mod_ref[0, 1:2, :]) + mod_ref[0, 0:1, :]
        a_scr[...] = h.astype(BF16)

    o_ref[...] = _dot(a_scr[...], w_ref[...])


def _in_projection(x, g, modg, w_bf16):
    R, D = x.shape
    N = w_bf16.shape[1]
    tm, tn = 512, 640
    return pl.pallas_call(
        _inproj_kernel,
        grid=(R // tm, N // tn),
        in_specs=[pl.BlockSpec((tm, D), lambda i, j: (i, 0)),
                  pl.BlockSpec((1, D), lambda i, j: (0, 0)),
                  pl.BlockSpec((1, 6, D), lambda i, j: (i * (tm // ROW_GROUP), 0, 0)),
                  pl.BlockSpec((D, tn), lambda i, j: (0, j))],
        out_specs=pl.BlockSpec((tm, tn), lambda i, j: (i, j)),
        out_shape=jax.ShapeDtypeStruct((R, N), F32),
        scratch_shapes=[pltpu.VMEM((tm, D), BF16)],
        compiler_params=_params(("parallel", "arbitrary"),
                                2 * tm * D * 4 + tm * D * 2 + 2 * D * tn * 2 + 4 * tm * tn * 4),
        name="in_projection",
    )(x, g.reshape(1, D), modg, w_bf16)


def _rope_block(blk, rope_ref):
    return (blk * rope_ref[0]
            + pltpu.roll(blk, LANE - MLA_ROPE // 2, 1) * rope_ref[1]
            + pltpu.roll(blk, MLA_ROPE // 2, 1) * rope_ref[2])


def _q_kernel(qdn_ref, g_ref, w_ref, rope_ref, o_ref):
    scale = (MLA_NOPE + MLA_ROPE) ** -0.5
    qn = _rms(qdn_ref[...], g_ref[...]).astype(BF16)
    q = _dot(qn, w_ref[...]) * scale
    parts = []
    for h in range(MLA_HEADS):
        base = h * HEAD_PAD
        parts.append(q[:, base:base + MLA_NOPE])
        parts.append(_rope_block(q[:, base + MLA_NOPE:base + HEAD_PAD], rope_ref))
    o_ref[...] = jnp.concatenate(parts, axis=1).astype(o_ref.dtype)


def _rope_index(tm, n_ctx_rows):
    n_ctx_tiles = n_ctx_rows // tm
    per_seq = DEC_SEQ // tm

    def index(i):
        return jnp.where(i < n_ctx_tiles, 0, 1 + (i - n_ctx_tiles) % per_seq)
    return index


def _q_projection(hin, g, w_bf16, rope_tab, n_ctx_rows):
    R = hin.shape[0]
    tm = 512
    N = MLA_HEADS * HEAD_PAD
    ridx = _rope_index(tm, n_ctx_rows)
    return pl.pallas_call(
        _q_kernel,
        grid=(R // tm,),
        in_specs=[pl.BlockSpec((tm, Q_LORA), lambda i: (i, COL_QDN // Q_LORA)),
                  pl.BlockSpec((1, Q_LORA), lambda i: (0, 0)),
                  pl.BlockSpec((Q_LORA, N), lambda i: (0, 0)),
                  pl.BlockSpec((3, tm, LANE), lambda i: (0, ridx(i), 0))],
        out_specs=pl.BlockSpec((tm, N), lambda i: (i, 0)),
        out_shape=jax.ShapeDtypeStruct((R, N), BF16),
        compiler_params=_params(("parallel",), 2 * Q_LORA * N * 2 + 6 * tm * N * 4),
        name="q_projection",
    )(hin, g.reshape(1, Q_LORA), w_bf16, rope_tab)


def _kv_kernel(kvdn_ref, kr_ref, g_ref, wk_ref, wv_ref, rope_ref, ckv_ref, k_ref, v_ref, *, normalize):
    c = kvdn_ref[...]
    if normalize:
        c = _rms(c, g_ref[...])
    ckv_ref[...] = c
    cb = c.astype(BF16)
    kn = _dot(cb, wk_ref[...])
    v_ref[...] = _dot(cb, wv_ref[...]).astype(v_ref.dtype)
    kr = _rope_block(kr_ref[...], rope_ref)
    parts = []
    for h in range(MLA_HEADS):
        parts.append(kn[:, h * MLA_NOPE:(h + 1) * MLA_NOPE])
        parts.append(kr)
    k_ref[...] = jnp.concatenate(parts, axis=1).astype(k_ref.dtype)


def _kv_projection(src, kv_col, kr_col, g, wk_bf16, wv_bf16, rope_tab, ridx, tm, normalize):
    R = src.shape[0]
    NK = MLA_HEADS * HEAD_PAD
    NV = MLA_HEADS * MLA_V
    return pl.pallas_call(
        functools.partial(_kv_kernel, normalize=normalize),
        grid=(R // tm,),
        in_specs=[pl.BlockSpec((tm, KV_LORA), lambda i: (i, kv_col)),
                  pl.BlockSpec((tm, LANE), lambda i: (i, kr_col)),
                  pl.BlockSpec((1, KV_LORA), lambda i: (0, 0)),
                  pl.BlockSpec((KV_LORA, MLA_HEADS * MLA_NOPE), lambda i: (0, 0)),
                  pl.BlockSpec((KV_LORA, NV), lambda i: (0, 0)),
                  pl.BlockSpec((3, tm, LANE), lambda i: (0, ridx(i), 0))],
        out_specs=[pl.BlockSpec((tm, KV_LORA), lambda i: (i, 0)),
                   pl.BlockSpec((tm, NK), lambda i: (i, 0)),
                   pl.BlockSpec((tm, NV), lambda i: (i, 0))],
        out_shape=[jax.ShapeDtypeStruct((R, KV_LORA), F32),
                   jax.ShapeDtypeStruct((R, NK), BF16),
                   jax.ShapeDtypeStruct((R, NV), BF16)],
        compiler_params=_params(("parallel",), 8 * tm * NK * 4),
        name="kv_projection" if normalize else "kv_projection_cache",
    )(src, src, g.reshape(1, KV_LORA), wk_bf16, wv_bf16, rope_tab)


def _attn_kernel(q_ref, k_ref, v_ref, o_ref):
    s = _dot_nt(q_ref[...], k_ref[...])
    m = jnp.max(s, axis=-1, keepdims=True)
    p = jnp.exp(s - m)
    l = jnp.sum(p, axis=-1, keepdims=True)
    o = _dot(p.astype(BF16), v_ref[...])
    o_ref[...] = (o / l).astype(o_ref.dtype)


def _attention(q, k, v, n_batch, tq_total, tk):
    tq = 256
    nq = tq_total // tq
    return pl.pallas_call(
        _attn_kernel,
        grid=(n_batch, MLA_HEADS, nq),
        in_specs=[pl.BlockSpec((tq, HEAD_PAD), lambda b, h, i: (b * nq + i, h)),
                  pl.BlockSpec((tk, HEAD_PAD), lambda b, h, i: (b, h)),
                  pl.BlockSpec((tk, MLA_V), lambda b, h, i: (b, h))],
        out_specs=pl.BlockSpec((tq, MLA_V), lambda b, h, i: (b * nq + i, h)),
        out_shape=jax.ShapeDtypeStruct((n_batch * tq_total, MLA_HEADS * MLA_V), BF16),
        compiler_params=_params(("parallel", "parallel", "arbitrary"), 8 * tq * tk * 4 + 4 * tk * HEAD_PAD * 2),
        name="attention",
    )(q, k, v)


def _fnet_kernel(z_ref, cs_ref, ct_ref, o_ref):
    yc, ys = [], []
    for g in range(FNET_GROUPS):
        zg = z_ref[:, g * FNET_GROUP_DIM:(g + 1) * FNET_GROUP_DIM].astype(BF16)
        y = _dot(zg, cs_ref[...])
        yc.append(y[:, :FNET_GROUP_DIM])
        ys.append(y[:, FNET_GROUP_DIM:])
    stacked = jnp.concatenate([jnp.concatenate(yc, axis=1), jnp.concatenate(ys, axis=1)], axis=0)
    o_ref[...] = _dot(ct_ref[...], stacked.astype(BF16)).astype(o_ref.dtype)


def _dft_tables(T):
    def cs(n):
        i = jnp.arange(n, dtype=jnp.int32)
        ang = ((i[:, None] * i[None, :]) % n).astype(F32) * (2.0 * math.pi / n)
        return jnp.cos(ang), jnp.sin(ang)
    cc, sc = cs(FNET_GROUP_DIM)
    ct, st = cs(T)
    norm = 1.0 / math.sqrt(T * FNET_GROUP_DIM)
    return (jnp.concatenate([cc, sc], axis=1).astype(BF16),
            (jnp.concatenate([ct, -st], axis=1) * norm).astype(BF16))


def _fourier_mix(hin, row0, n_batch, T):
    cs_tab, ct_tab = _dft_tables(T)
    b0 = row0 // T
    return pl.pallas_call(
        _fnet_kernel,
        grid=(n_batch,),
        in_specs=[pl.BlockSpec((T, FNET_WIDTH), lambda b: (b0 + b, COL_XF // FNET_WIDTH)),
                  pl.BlockSpec((FNET_GROUP_DIM, 2 * FNET_GROUP_DIM), lambda b: (0, 0)),
                  pl.BlockSpec((T, 2 * T), lambda b: (0, 0))],
        out_specs=pl.BlockSpec((T, FNET_WIDTH), lambda b: (b, 0)),
        out_shape=jax.ShapeDtypeStruct((n_batch * T, FNET_WIDTH), BF16),
        compiler_params=_params(("parallel",), 2 * T * 2 * T * 2 + 10 * T * FNET_WIDTH * 4),
        name="fourier_mix",
    )(hin, cs_tab, ct_tab)


def _head_sum_matrix(scale):
    i = np.arange(RWKV_WIDTH) // RWKV_HEAD_DIM
    return jnp.asarray((i[:, None] == i[None, :]).astype(np.float32) * scale, BF16)


def _rwkv_prep_kernel(x_ref, xp_ref, xn_ref, wa_ref, gl_ref, cw_ref, w0_ref, w2_ref, a0_ref, a2_ref, g2_ref,
                      kk_ref, ka_ref, rk_ref, hs_ref,
                      r_ref, kh_ref, v_ref, g_ref, bonus_ref, lw_ref, kt_ref, beta_ref,
                      *, n_ctx_tiles, tiles_per_seq):
    i = pl.program_id(0)
    j = i - n_ctx_tiles
    first = jnp.logical_or(i < n_ctx_tiles, j % tiles_per_seq == 0)
    last = jnp.logical_or(i < n_ctx_tiles, j % tiles_per_seq == tiles_per_seq - 1)
    x = x_ref[...]
    tm = x.shape[0]
    row = lax.broadcasted_iota(jnp.int32, x.shape, 0)
    prev_row = jnp.where(first, 0.0, xp_ref[7:8, :])
    next_row = jnp.where(last, 0.0, xn_ref[0:1, :])
    x_prev = jnp.where(row == 0, prev_row, pltpu.roll(x, 1, 0))
    x_next = jnp.where(row == tm - 1, next_row, pltpu.roll(x, tm - 1, 0))
    xc = x_prev * cw_ref[0:1, :] + x * cw_ref[1:2, :] + x_next * cw_ref[2:3, :]
    R_ = RWKV_WIDTH
    r, k, v = xc[:, :R_], xc[:, R_:2 * R_], xc[:, 2 * R_:]
    wl = jnp.tanh(wa_ref[:, :W_LORA]).astype(BF16)
    al = wa_ref[:, W_LORA:W_LORA + A_LORA].astype(BF16)
    g_ref[...] = _dot(_sigmoid(gl_ref[...]).astype(BF16), g2_ref[...].astype(BF16))
    kappa = k * kk_ref[...]
    ss = _dot_exact_rhs(kappa * kappa, hs_ref[...], 3)
    kh = kappa * lax.rsqrt(ss + EPS)
    kt_sum = None
    for d in range(2):
        lw_ref[d] = -DECAY_SCALE * _sigmoid(w0_ref[d:d + 1, :] + _dot(wl, w2_ref[d].astype(BF16)))
        a = _sigmoid(a0_ref[d:d + 1, :] + _dot(al, a2_ref[d].astype(BF16)))
        kt = k * (1.0 + (a - 1.0) * ka_ref[...])
        kt_ref[d] = kt
        beta_ref[d] = a * kh
        kt_sum = kt if kt_sum is None else kt_sum + kt
    r_ref[...] = r
    kh_ref[...] = kh
    v_ref[...] = v
    bonus_ref[...] = _dot_exact_rhs(r * kt_sum * rk_ref[...], hs_ref[...], 3) * v


def _rwkv_prep(hin, p, n_ctx_rows):
    R = hin.shape[0]
    tm = ROW_GROUP
    W3 = 3 * RWKV_WIDTH
    nh = tm // 8
    last_blk = R // 8 - 1
    row = lambda a: a.reshape(1, RWKV_WIDTH)
    full = lambda *s: pl.BlockSpec(s, lambda i: (0,) * len(s))
    out_rw = jax.ShapeDtypeStruct((R, RWKV_WIDTH), F32)
    out_rw2 = jax.ShapeDtypeStruct((2, R, RWKV_WIDTH), F32)
    spec_rw = pl.BlockSpec((tm, RWKV_WIDTH), lambda i: (i, 0))
    spec_rw2 = pl.BlockSpec((2, tm, RWKV_WIDTH), lambda i: (0, i, 0))
    return pl.pallas_call(
        functools.partial(_rwkv_prep_kernel, n_ctx_tiles=n_ctx_rows // tm, tiles_per_seq=DEC_SEQ // tm),
        grid=(R // tm,),
        in_specs=[pl.BlockSpec((tm, W3), lambda i: (i, COL_RKV // W3)),
                  pl.BlockSpec((8, W3), lambda i: (jnp.maximum(i * nh - 1, 0), 0)),
                  pl.BlockSpec((8, W3), lambda i: (jnp.minimum((i + 1) * nh, last_blk), 0)),
                  pl.BlockSpec((tm, LANE), lambda i: (i, COL_WA // LANE)),
                  pl.BlockSpec((tm, G_LORA), lambda i: (i, COL_GLO // G_LORA)),
                  full(3, W3), full(2, RWKV_WIDTH), full(2, W_LORA, RWKV_WIDTH),
                  full(2, RWKV_WIDTH), full(2, A_LORA, RWKV_WIDTH), full(G_LORA, RWKV_WIDTH),
                  full(1, RWKV_WIDTH), full(1, RWKV_WIDTH), full(1, RWKV_WIDTH),
                  full(RWKV_WIDTH, RWKV_WIDTH)],
        out_specs=[spec_rw] * 5 + [spec_rw2] * 3,
        out_shape=[out_rw] * 5 + [out_rw2] * 3,
        compiler_params=_params(("parallel",), 40 * tm * W3 * 4),
        name="rwkv_prep",
    )(hin, hin, hin, hin, hin, p['rwkv_conv'], p['rwkv_w0'], p['rwkv_w2'], p['rwkv_a0'], p['rwkv_a2'],
      p['rwkv_g2'], row(p['rwkv_k_k']), row(p['rwkv_k_a']), row(p['rwkv_r_k']), _head_sum_matrix(1.0))


def _rwkv_scan_kernel(r_ref, kh_ref, v_ref, lw_ref, kt_ref, beta_ref, mi_ref, s0_ref, y_ref, sout_ref, s_scr):
    c = pl.program_id(2)
    nc = pl.num_programs(2)
    N = RWKV_HEAD_DIM
    C = CHUNK

    @pl.when(c == 0)
    def _():
        s_scr[...] = s0_ref[0, 0]

    mi = mi_ref[0]
    eye = (lax.broadcasted_iota(jnp.int32, (C, C), 0) == lax.broadcasted_iota(jnp.int32, (C, C), 1)).astype(F32)
    ms = mi - eye
    lw = lw_ref[0]
    cl = _dot_exact_rhs_lhs(mi.astype(BF16), lw)
    cl_tot = jnp.sum(lw, axis=0, keepdims=True)
    g_ex = jnp.exp(cl - lw)
    g_in = jnp.exp(cl)
    g_inv = jnp.exp(-cl)
    g_end = jnp.exp(cl_tot - cl)
    g_tot = jnp.exp(cl_tot)
    kh = kh_ref[...]
    beta = beta_ref[0]
    kt = kt_ref[0]
    k_dec = kh * g_ex
    r_dec = r_ref[...] * g_in
    b_inv = beta * g_inv
    k_inv = kt * g_inv
    k_end = kt * g_end
    b_end = beta * g_end
    v_all = v_ref[...]
    ys = []
    for h in range(RWKV_HEADS):
        sl = slice(h * N, (h + 1) * N)
        s0 = s_scr[h]
        kd, rd, bi, ki = k_dec[:, sl], r_dec[:, sl], b_inv[:, sl], k_inv[:, sl]
        vv = v_all[:, sl]
        lhs = jnp.concatenate([kd, rd], axis=0).astype(BF16)
        rhs = jnp.concatenate([bi, ki], axis=0).astype(BF16)
        nt = _dot_nt(lhs, rhs)
        a_m = nt[:C, :C] * ms
        b_m = nt[:C, C:] * ms
        q_m = nt[C:, :C] * mi
        p_m = nt[C:, C:] * mi
        xp = (-a_m).astype(BF16)
        tm_ = eye - a_m
        for _ in range(int(math.log2(C)) - 1):
            xp32 = _dot(xp, xp)
            xp = xp32.astype(BF16)
            tm_ = tm_ + _dot(tm_.astype(BF16), xp)
        z = _dot_x3(kd, s0, _dot_nt) + _dot_x3(b_m, vv)
        u = _dot_x3(tm_, z)
        y = _dot_nt(rd.astype(BF16), s0.astype(BF16)) + _dot(p_m.astype(BF16), vv.astype(BF16)) \
            - _dot(q_m.astype(BF16), u.astype(BF16))
        ys.append(y)
        s_new = s0 * g_tot[:, sl] + _dot_x3(vv, k_end[:, sl], _dot_tn) - _dot_x3(u, b_end[:, sl], _dot_tn)
        s_scr[h] = s_new
    y_ref[0] = jnp.concatenate(ys, axis=1)

    @pl.when(c == nc - 1)
    def _():
        sout_ref[0, 0] = s_scr[...]


def _dot_exact_rhs_lhs(a_exact_bf16, b):
    acc = None
    for t in _split_bf16(b, 3):
        p = _dot(a_exact_bf16, t)
        acc = p if acc is None else acc + p
    return acc


def _causal_masks():
    t = np.arange(CHUNK)
    fwd = (t[None, :] <= t[:, None]).astype(np.float32)
    return jnp.asarray(np.stack([fwd, fwd.T]))


def _rwkv_scan(prep, row0, n_batch, T, s0):
    r, kh, v, _, _, lw, kt, beta = prep
    C = CHUNK
    nc = T // C
    c0 = row0 // C
    W = RWKV_WIDTH

    def blk(b, d, c):
        return c0 + b * nc + c + d * (nc - 1 - 2 * c)

    def blk_out(b, d, c):
        return b * nc + c + d * (nc - 1 - 2 * c)

    shared = pl.BlockSpec((C, W), lambda b, d, c: (blk(b, d, c), 0))
    per_dir = pl.BlockSpec((1, C, W), lambda b, d, c: (d, blk(b, d, c), 0))
    st = pl.BlockSpec((1, 1, RWKV_HEADS, RWKV_HEAD_DIM, RWKV_HEAD_DIM), lambda b, d, c: (b, d, 0, 0, 0))
    return pl.pallas_call(
        _rwkv_scan_kernel,
        grid=(n_batch, 2, nc),
        in_specs=[shared, shared, shared, per_dir, per_dir, per_dir,
                  pl.BlockSpec((1, C, C), lambda b, d, c: (d, 0, 0)), st],
        out_specs=[pl.BlockSpec((1, C, W), lambda b, d, c: (d, blk_out(b, d, c), 0)), st],
        out_shape=[jax.ShapeDtypeStruct((2, n_batch * T, W), F32),
                   jax.ShapeDtypeStruct(s0.shape, F32)],
        scratch_shapes=[pltpu.VMEM((RWKV_HEADS, RWKV_HEAD_DIM, RWKV_HEAD_DIM), F32)],
        compiler_params=_params(("parallel", "parallel", "arbitrary"), 32 * 1024 * 1024),
        name="rwkv_scan",
    )(r, kh, v, lw, kt, beta, _causal_masks(), s0)


def _rwkv_fin_kernel(y_ref, bonus_ref, g_ref, gg_ref, gb_ref, hm_ref, o_ref):
    y = y_ref[0] + y_ref[1]
    mu = _dot_exact_rhs(y, hm_ref[...], 3)
    yc = y - mu
    var = _dot_exact_rhs(yc * yc, hm_ref[...], 3)
    yn = yc * lax.rsqrt(var + GN_EPS) * gg_ref[...] + gb_ref[...]
    o_ref[...] = ((yn + bonus_ref[...]) * g_ref[...]).astype(o_ref.dtype)


def _rwkv_finalize(y, bonus, g, gn_g, gn_b):
    R = y.shape[1]
    tm = 512
    W = RWKV_WIDTH
    rows = pl.BlockSpec((tm, W), lambda i: (i, 0))
    vec = pl.BlockSpec((1, W), lambda i: (0, 0))
    return pl.pallas_call(
        _rwkv_fin_kernel,
        grid=(R // tm,),
        in_specs=[pl.BlockSpec((2, tm, W), lambda i: (0, i, 0)), rows, rows, vec, vec,
                  pl.BlockSpec((W, W), lambda i: (0, 0))],
        out_specs=rows,
        out_shape=jax.ShapeDtypeStruct((R, W), BF16),
        compiler_params=_params(("parallel",), 24 * tm * W * 4),
        name="rwkv_finalize",
    )(y, bonus, g, gn_g.reshape(1, W), gn_b.reshape(1, W), _head_sum_matrix(1.0 / RWKV_HEAD_DIM))


def _outproj_kernel(attn_ref, four_ref, rw_ref, w_ref, x_ref, g_ref, mod_ref, o_ref):
    a0 = MLA_HEADS * MLA_V
    a1 = a0 + FNET_WIDTH
    acc = _dot(attn_ref[...], w_ref[0:a0, :])
    acc += _dot(four_ref[...], w_ref[a0:a1, :])
    acc += _dot(rw_ref[...], w_ref[a1:, :])
    o_ref[...] = x_ref[...] + mod_ref[0, 2:3, :] * _rms(acc, g_ref[...])


def _out_projection(attn, four, rw, w_bf16, x, g, modg):
    R, D = x.shape
    tm = 256
    K = w_bf16.shape[0]
    return pl.pallas_call(
        _outproj_kernel,
        grid=(R // tm,),
        in_specs=[pl.BlockSpec((tm, attn.shape[1]), lambda i: (i, 0)),
                  pl.BlockSpec((tm, four.shape[1]), lambda i: (i, 0)),
                  pl.BlockSpec((tm, rw.shape[1]), lambda i: (i, 0)),
                  pl.BlockSpec((K, D), lambda i: (0, 0)),
                  pl.BlockSpec((tm, D), lambda i: (i, 0)),
                  pl.BlockSpec((1, D), lambda i: (0, 0)),
                  pl.BlockSpec((1, 6, D), lambda i: (i * (tm // ROW_GROUP), 0, 0))],
        out_specs=pl.BlockSpec((tm, D), lambda i: (i, 0)),
        out_shape=jax.ShapeDtypeStruct((R, D), F32),
        compiler_params=_params(("parallel",), 2 * K * D * 2 + 8 * tm * D * 4),
        name="out_projection",
    )(attn, four, rw, w_bf16, x, g.reshape(1, D), modg)


def _ffn_up_kernel(x_ref, g_ref, mod_ref, wg_ref, wv_ref, cwg_ref, cwv_ref, cbg_ref, cbv_ref, o_ref, a_scr,
                   *, n_ctx_tiles):
    @pl.when(pl.program_id(1) == 0)
    def _():
        h = _rms(x_ref[...], g_ref[...]) * (1.0 + mod_ref[0, 4:5, :]) + mod_ref[0, 3:4, :]
        a_scr[...] = h.astype(BF16)

    tm = a_scr.shape[0]
    seq = jnp.where(pl.program_id(0) < n_ctx_tiles, SEQ, DEC_SEQ)
    a = a_scr[...]

    def conv(u, cw_ref, cb_ref):
        pos = lax.broadcasted_iota(jnp.int32, u.shape, 0) & (seq - 1)
        u_prev = jnp.where(pos == 0, 0.0, pltpu.roll(u, 1, 0))
        u_next = jnp.where(pos == seq - 1, 0.0, pltpu.roll(u, tm - 1, 0))
        return u_prev * cw_ref[0:1, :] + u * cw_ref[1:2, :] + u_next * cw_ref[2:3, :] + cb_ref[...]

    gate = conv(_dot(a, wg_ref[...]), cwg_ref, cbg_ref)
    val = conv(_dot(a, wv_ref[...]), cwv_ref, cbv_ref)
    o_ref[...] = (gate * _sigmoid(gate) * val).astype(o_ref.dtype)


def _ffn_up(x, g, modg, w_bf16, conv_w, conv_b, n_ctx_rows):
    R, D = x.shape
    tm, tn = DEC_SEQ, 512
    nj = D_FF // tn
    cb = conv_b.reshape(1, 2 * D_FF)
    return pl.pallas_call(
        functools.partial(_ffn_up_kernel, n_ctx_tiles=n_ctx_rows // tm),
        grid=(R // tm, nj),
        in_specs=[pl.BlockSpec((tm, D), lambda i, j: (i, 0)),
                  pl.BlockSpec((1, D), lambda i, j: (0, 0)),
                  pl.BlockSpec((1, 6, D), lambda i, j: (i * (tm // ROW_GROUP), 0, 0)),
                  pl.BlockSpec((D, tn), lambda i, j: (0, j)),
                  pl.BlockSpec((D, tn), lambda i, j: (0, j + nj)),
                  pl.BlockSpec((3, tn), lambda i, j: (0, j)),
                  pl.BlockSpec((3, tn), lambda i, j: (0, j + nj)),
                  pl.BlockSpec((1, tn), lambda i, j: (0, j)),
                  pl.BlockSpec((1, tn), lambda i, j: (0, j + nj))],
        out_specs=pl.BlockSpec((tm, tn), lambda i, j: (i, j)),
        out_shape=jax.ShapeDtypeStruct((R, D_FF), BF16),
        scratch_shapes=[pltpu.VMEM((tm, D), BF16)],
        compiler_params=_params(("parallel", "arbitrary"),
                                2 * tm * D * 4 + tm * D * 2 + 4 * D * tn * 2 + 12 * tm * tn * 4),
        name="ffn_up",
    )(x, g.reshape(1, D), modg, w_bf16, w_bf16, conv_w, conv_w, cb, cb)


def _ffn_down_kernel(a_ref, w_ref, x_ref, g_ref, mod_ref, o_ref, acc_scr):
    k = pl.program_id(1)

    @pl.when(k == 0)
    def _():
        acc_scr[...] = jnp.zeros_like(acc_scr)

    acc_scr[...] += _dot(a_ref[...], w_ref[...])

    @pl.when(k == pl.num_programs(1) - 1)
    def _():
        o_ref[...] = x_ref[...] + mod_ref[0, 5:6, :] * _rms(acc_scr[...], g_ref[...])


def _ffn_down(act, w_bf16, x, g, modg):
    R, D = x.shape
    K = act.shape[1]
    tm, tk = 512, 512
    return pl.pallas_call(
        _ffn_down_kernel,
        grid=(R // tm, K // tk),
        in_specs=[pl.BlockSpec((tm, tk), lambda i, k: (i, k)),
                  pl.BlockSpec((tk, D), lambda i, k: (k, 0)),
                  pl.BlockSpec((tm, D), lambda i, k: (i, 0)),
                  pl.BlockSpec((1, D), lambda i, k: (0, 0)),
                  pl.BlockSpec((1, 6, D), lambda i, k: (i * (tm // ROW_GROUP), 0, 0))],
        out_specs=pl.BlockSpec((tm, D), lambda i, k: (i, 0)),
        out_shape=jax.ShapeDtypeStruct((R, D), F32),
        scratch_shapes=[pltpu.VMEM((tm, D), F32)],
        compiler_params=_params(("parallel", "arbitrary"), 8 * tm * D * 4 + 2 * tk * D * 2),
        name="ffn_down",
    )(act, w_bf16, x, g.reshape(1, D), modg)


def _layout_w_in(w):
    o = np.cumsum([0, Q_LORA, KV_LORA, MLA_ROPE, FNET_WIDTH, 3 * RWKV_WIDTH, W_LORA, A_LORA, G_LORA])
    q_dn, kv_dn, k_rope, xf, rkv, w_lo, a_lo, g_lo = [w[:, o[i]:o[i + 1]] for i in range(8)]
    zeros = jnp.zeros((w.shape[0], LANE - MLA_ROPE), w.dtype)
    return jnp.concatenate([rkv, q_dn, xf, kv_dn, k_rope, zeros, w_lo, a_lo, g_lo], axis=1).astype(BF16)


def _layout_w_uq(w):
    w = w.reshape(Q_LORA, MLA_HEADS, MLA_NOPE + MLA_ROPE)
    w = jnp.pad(w, ((0, 0), (0, 0), (0, HEAD_PAD - MLA_NOPE - MLA_ROPE)))
    return w.reshape(Q_LORA, MLA_HEADS * HEAD_PAD).astype(BF16)


def _layout_w_ukv(w):
    w = w.reshape(KV_LORA, MLA_HEADS, MLA_NOPE + MLA_V)
    wk = w[:, :, :MLA_NOPE].reshape(KV_LORA, MLA_HEADS * MLA_NOPE)
    wv = w[:, :, MLA_NOPE:].reshape(KV_LORA, MLA_HEADS * MLA_V)
    return wk.astype(BF16), wv.astype(BF16)


def _rope_table(tm):
    t = jnp.arange(DEC_SEQ)
    rowp = (t // GRID_W).astype(F32)
    colp = (t % GRID_W).astype(F32)
    n = MLA_ROPE // 4
    inv = ROPE_BASE ** (-jnp.arange(n, dtype=F32) / n)
    ang = jnp.concatenate([rowp[:, None] * inv, colp[:, None] * inv], axis=-1)
    cos = jnp.concatenate([jnp.ones((tm, MLA_ROPE // 2), F32), jnp.cos(ang)], axis=0)
    sin = jnp.concatenate([jnp.zeros((tm, MLA_ROPE // 2), F32), jnp.sin(ang)], axis=0)
    z32 = jnp.zeros_like(cos)
    z64 = jnp.zeros((cos.shape[0], LANE - MLA_ROPE), F32)
    mul = jnp.concatenate([cos, cos, z64], axis=1)
    left = jnp.concatenate([-sin, z32, z64], axis=1)
    right = jnp.concatenate([z32, sin, z64], axis=1)
    return jnp.stack([mul, left, right])


def kernel(x_prompt, x_sample, cache_mla_ckv, cache_mla_krope, state_rwkv, c, c_ctx, w_mod, b_mod, g_pre_mix, g_post_mix, g_pre_ffn, g_post_ffn, w_in, g_q_norm, w_uq, g_kv_norm, w_ukv, rwkv_conv, rwkv_w0, rwkv_w2, rwkv_a0, rwkv_a2, rwkv_g2, rwkv_k_k, rwkv_k_a, rwkv_r_k, rwkv_gn_g, rwkv_gn_b, w_out, ffn_w_up, ffn_conv, ffn_conv_b, ffn_w_down):
    Bc, Tc, D = x_prompt.shape
    Bl, Tl, _ = x_sample.shape
    L = w_mod.shape[0]
    assert (Tc, Tl, D) == (SEQ, DEC_SEQ, D_MODEL) and cache_mla_ckv.shape[2] == PAST_LEN
    assert Bl + 1 <= 8 and (Bc * Tc) % DEC_SEQ == 0
    n_ctx = Bc * Tc
    n_lat = Bl * Tl
    x = jnp.concatenate([x_prompt.reshape(n_ctx, D), x_sample.reshape(n_lat, D)], axis=0)

    cond = jnp.zeros((8, D), F32).at[0].set(c_ctx).at[1:1 + Bl].set(c)
    mods = _modulation(cond, w_mod, b_mod)
    group_row = np.concatenate([np.zeros(n_ctx // ROW_GROUP, np.int32),
                                1 + np.arange(n_lat // ROW_GROUP, dtype=np.int32) // (Tl // ROW_GROUP)])

    tm_mla = 512
    rope_tab = _rope_table(tm_mla)
    ridx = _rope_index(tm_mla, n_ctx)
    ridx_cache = lambda i: 0
    zero_state = jnp.zeros((Bc, 2, RWKV_HEADS, RWKV_HEAD_DIM, RWKV_HEAD_DIM), F32)
    tk_lat = PAST_LEN + Tl

    ckv_out, krope_out, state_out = [], [], []
    for l in range(L):
        modg = mods[l][group_row].reshape(-1, 6, D)
        wk, wv = _layout_w_ukv(w_ukv[l])
        p = dict(rwkv_conv=rwkv_conv[l], rwkv_w0=rwkv_w0[l], rwkv_w2=rwkv_w2[l], rwkv_a0=rwkv_a0[l],
                 rwkv_a2=rwkv_a2[l], rwkv_g2=rwkv_g2[l], rwkv_k_k=rwkv_k_k[l], rwkv_k_a=rwkv_k_a[l],
                 rwkv_r_k=rwkv_r_k[l])

        hin = _in_projection(x, g_pre_mix[l], modg, _layout_w_in(w_in[l]))

        q = _q_projection(hin, g_q_norm[l], _layout_w_uq(w_uq[l]), rope_tab, n_ctx)
        ckv, k, v = _kv_projection(hin, COL_KVDN // KV_LORA, COL_KROPE // LANE, g_kv_norm[l], wk, wv,
                                   rope_tab, ridx, tm_mla, True)
        cache_src = jnp.concatenate(
            [cache_mla_ckv[:, l].reshape(Bl * PAST_LEN, KV_LORA),
             jnp.pad(cache_mla_krope[:, l].reshape(Bl * PAST_LEN, MLA_ROPE), ((0, 0), (0, LANE - MLA_ROPE)))],
            axis=1)
        _, k_c, v_c = _kv_projection(cache_src, 0, KV_LORA // LANE, g_kv_norm[l], wk, wv,
                                     rope_tab, ridx_cache, PAST_LEN, False)
        attn_ctx = _attention(q[:n_ctx], k[:n_ctx], v[:n_ctx], Bc, Tc, Tc)
        k_lat = jnp.concatenate([k_c.reshape(Bl, PAST_LEN, -1), k[n_ctx:].reshape(Bl, Tl, -1)], axis=1)
        v_lat = jnp.concatenate([v_c.reshape(Bl, PAST_LEN, -1), v[n_ctx:].reshape(Bl, Tl, -1)], axis=1)
        attn_lat = _attention(q[n_ctx:], k_lat.reshape(Bl * tk_lat, -1), v_lat.reshape(Bl * tk_lat, -1),
                              Bl, Tl, tk_lat)
        attn = jnp.concatenate([attn_ctx, attn_lat], axis=0)

        four = jnp.concatenate([_fourier_mix(hin, 0, Bc, Tc), _fourier_mix(hin, n_ctx, Bl, Tl)], axis=0)

        prep = _rwkv_prep(hin, p, n_ctx)
        y_ctx, s_ctx = _rwkv_scan(prep, 0, Bc, Tc, zero_state)
        y_lat, _ = _rwkv_scan(prep, n_ctx, Bl, Tl, state_rwkv[:, l])
        rw = _rwkv_finalize(jnp.concatenate([y_ctx, y_lat], axis=1), prep[4], prep[3],
                            rwkv_gn_g[l], rwkv_gn_b[l])

        x = _out_projection(attn, four, rw, w_out[l].astype(BF16), x, g_post_mix[l], modg)
        act = _ffn_up(x, g_pre_ffn[l], modg, ffn_w_up[l].astype(BF16), ffn_conv[l], ffn_conv_b[l], n_ctx)
        x = _ffn_down(act, ffn_w_down[l].astype(BF16), x, g_post_ffn[l], modg)

        ckv_out.append(ckv[:n_ctx].reshape(Bc, Tc, KV_LORA))
        krope_out.append(hin[:n_ctx, COL_KROPE:COL_KROPE + MLA_ROPE].reshape(Bc, Tc, MLA_ROPE))
        state_out.append(s_ctx)

    return (x[:n_ctx].reshape(Bc, Tc, D), x[n_ctx:].reshape(Bl, Tl, D),
            jnp.stack(ckv_out, axis=1), jnp.stack(krope_out, axis=1), jnp.stack(state_out, axis=1))
```

```python
import functools
import math

import jax
import jax.numpy as jnp
import numpy as np
from jax import lax
from jax.experimental import pallas as pl
from jax.experimental.pallas import tpu as pltpu

F32 = jnp.float32
BF16 = jnp.bfloat16

D_MODEL = 2048
SEQ = 256
DEC_SEQ = 1024
PAST_LEN = 256
GRID_W = 64
MLA_HEADS = 8
MLA_NOPE = 128
MLA_ROPE = 64
MLA_V = 128
Q_LORA = 512
KV_LORA = 256
FNET_GROUPS = 4
FNET_GROUP_DIM = 128
FNET_WIDTH = FNET_GROUPS * FNET_GROUP_DIM
RWKV_HEADS = 8
RWKV_HEAD_DIM = 64
RWKV_WIDTH = RWKV_HEADS * RWKV_HEAD_DIM
W_LORA = 64
A_LORA = 64
G_LORA = 128
D_FF = 5632
ROPE_BASE = 10000.0
EPS = 1e-6
GN_EPS = 64e-5
DECAY_SCALE = math.exp(-0.5)

LANE = 128
ROW_GROUP = 256
HEAD_PAD = 256
CHUNK = 64
VMEM_CAP = 56 * 1024 * 1024

COL_RKV = 0
COL_QDN = 3 * RWKV_WIDTH
COL_XF = COL_QDN + Q_LORA
COL_KVDN = COL_XF + FNET_WIDTH
COL_KROPE = COL_KVDN + KV_LORA
COL_WA = COL_KROPE + LANE
COL_GLO = COL_WA + LANE
IN_PAD = COL_GLO + G_LORA


SMALL_OPERANDS = 8 * 1024 * 1024


def _params(sem, nbytes):
    return pltpu.CompilerParams(dimension_semantics=sem,
                                vmem_limit_bytes=int(min(VMEM_CAP, nbytes + SMALL_OPERANDS)))


def _dot(a, b):
    return jnp.dot(a, b, preferred_element_type=F32)


def _dot_nt(a, b):
    return lax.dot_general(a, b, (((1,), (1,)), ((), ())), preferred_element_type=F32)


def _dot_tn(a, b):
    return lax.dot_general(a, b, (((0,), (0,)), ((), ())), preferred_element_type=F32)


def _split_bf16(x, n):
    terms = []
    rem = x
    for i in range(n):
        t = rem.astype(BF16)
        terms.append(t)
        if i + 1 < n:
            rem = rem - t.astype(F32)
    return terms


def _dot_exact_rhs(a, b_exact_bf16, n):
    acc = None
    for t in _split_bf16(a, n):
        p = _dot(t, b_exact_bf16)
        acc = p if acc is None else acc + p
    return acc


def _rms(x, g):
    return x * lax.rsqrt(jnp.mean(x * x, axis=-1, keepdims=True) + EPS) * g


def _sigmoid(x):
    return 1.0 / (1.0 + jnp.exp(-x))


def _mod_kernel(cond_ref, w_ref, b_ref, o_ref):
    c = cond_ref[...]
    s = (c * _sigmoid(c)).astype(BF16)
    o_ref[0] = _dot(s, w_ref[0].astype(BF16)) + b_ref[0]


def _modulation(cond, w_mod, b_mod):
    L, D, N = w_mod.shape
    tn = 1024
    return pl.pallas_call(
        _mod_kernel,
        grid=(L, N // tn),
        in_specs=[pl.BlockSpec((8, D), lambda l, j: (0, 0)),
                  pl.BlockSpec((1, D, tn), lambda l, j: (l, 0, j)),
                  pl.BlockSpec((1, 1, tn), lambda l, j: (l, 0, j))],
        out_specs=pl.BlockSpec((1, 8, tn), lambda l, j: (l, 0, j)),
        out_shape=jax.ShapeDtypeStruct((L, 8, N), F32),
        compiler_params=_params(("parallel", "parallel"), 3 * D * tn * 4),
        name="modulation",
    )(cond, w_mod, b_mod.reshape(L, 1, N))


def _inproj_kernel(x_ref, g_ref, mod_ref, w_ref, o_ref, a_scr):
    @pl.when(pl.program_id(1) == 0)
    def _():
        h = _rms(x_ref[...], g_ref[...]) * (1.0 + mod_ref[0, 1:2, :]) + mod_ref[0, 0:1, :]
        a_scr[...] = h.astype(BF16)

    o_ref[...] = _dot(a_scr[...], w_ref[...])


def _in_projection(x, g, modg, w_bf16):
    R, D = x.shape
    N = w_bf16.shape[1]
    tm, tn = 1024, 640
    return pl.pallas_call(
        _inproj_kernel,
        grid=(R // tm, N // tn),
        in_specs=[pl.BlockSpec((tm, D), lambda i, j: (i, 0)),
                  pl.BlockSpec((1, D), lambda i, j: (0, 0)),
                  pl.BlockSpec((1, 6, D), lambda i, j: (i * (tm // ROW_GROUP), 0, 0)),
                  pl.BlockSpec((D, tn), lambda i, j: (0, j))],
        out_specs=pl.BlockSpec((tm, tn), lambda i, j: (i, j)),
        out_shape=jax.ShapeDtypeStruct((R, N), F32),
        scratch_shapes=[pltpu.VMEM((tm, D), BF16)],
        compiler_params=_params(("parallel", "arbitrary"),
                                2 * tm * D * 4 + tm * D * 2 + 2 * D * tn * 2 + 4 * tm * tn * 4),
        name="in_projection",
    )(x, g.reshape(1, D), modg, w_bf16)


def _rope_block(blk, rope_ref):
    return (blk * rope_ref[0]
            + pltpu.roll(blk, LANE - MLA_ROPE // 2, 1) * rope_ref[1]
            + pltpu.roll(blk, MLA_ROPE // 2, 1) * rope_ref[2])


def _q_kernel(qdn_ref, g_ref, w_ref, rope_ref, o_ref):
    scale = (MLA_NOPE + MLA_ROPE) ** -0.5
    qn = _rms(qdn_ref[...], g_ref[...]).astype(BF16)
    q = _dot(qn, w_ref[...]) * scale
    parts = []
    for h in range(MLA_HEADS):
        base = h * HEAD_PAD
        parts.append(q[:, base:base + MLA_NOPE])
        parts.append(_rope_block(q[:, base + MLA_NOPE:base + HEAD_PAD], rope_ref))
    o_ref[...] = jnp.concatenate(parts, axis=1).astype(o_ref.dtype)


def _rope_index(tm, n_ctx_rows):
    n_ctx_tiles = n_ctx_rows // tm
    per_seq = DEC_SEQ // tm

    def index(i):
        return jnp.where(i < n_ctx_tiles, 0, 1 + (i - n_ctx_tiles) % per_seq)
    return index


def _q_projection(hin, g, w_bf16, rope_tab, n_ctx_rows):
    R = hin.shape[0]
    tm = 512
    N = MLA_HEADS * HEAD_PAD
    ridx = _rope_index(tm, n_ctx_rows)
    return pl.pallas_call(
        _q_kernel,
        grid=(R // tm,),
        in_specs=[pl.BlockSpec((tm, Q_LORA), lambda i: (i, COL_QDN // Q_LORA)),
                  pl.BlockSpec((1, Q_LORA), lambda i: (0, 0)),
                  pl.BlockSpec((Q_LORA, N), lambda i: (0, 0)),
                  pl.BlockSpec((3, tm, LANE), lambda i: (0, ridx(i), 0))],
        out_specs=pl.BlockSpec((tm, N), lambda i: (i, 0)),
        out_shape=jax.ShapeDtypeStruct((R, N), BF16),
        compiler_params=_params(("parallel",), 2 * Q_LORA * N * 2 + 6 * tm * N * 4),
        name="q_projection",
    )(hin, g.reshape(1, Q_LORA), w_bf16, rope_tab)


def _kv_kernel(kvdn_ref, kr_ref, g_ref, wk_ref, wv_ref, rope_ref, *out_refs, normalize):
    k_ref, v_ref = out_refs[-2:]
    c = kvdn_ref[...]
    if normalize:
        c = _rms(c, g_ref[...])
        out_refs[0][...] = c
    cb = c.astype(BF16)
    kn = _dot(cb, wk_ref[...])
    v_ref[...] = _dot(cb, wv_ref[...]).astype(v_ref.dtype)
    kr = _rope_block(kr_ref[...], rope_ref)
    parts = []
    for h in range(MLA_HEADS):
        parts.append(kn[:, h * MLA_NOPE:(h + 1) * MLA_NOPE])
        parts.append(kr)
    k_ref[...] = jnp.concatenate(parts, axis=1).astype(k_ref.dtype)


def _kv_projection(src, kv_col, kr_col, g, wk_bf16, wv_bf16, rope_tab, ridx, tm, normalize):
    R = src.shape[0]
    NK = MLA_HEADS * HEAD_PAD
    NV = MLA_HEADS * MLA_V
    return pl.pallas_call(
        functools.partial(_kv_kernel, normalize=normalize),
        grid=(R // tm,),
        in_specs=[pl.BlockSpec((tm, KV_LORA), lambda i: (i, kv_col)),
                  pl.BlockSpec((tm, LANE), lambda i: (i, kr_col)),
                  pl.BlockSpec((1, KV_LORA), lambda i: (0, 0)),
                  pl.BlockSpec((KV_LORA, MLA_HEADS * MLA_NOPE), lambda i: (0, 0)),
                  pl.BlockSpec((KV_LORA, NV), lambda i: (0, 0)),
                  pl.BlockSpec((3, tm, LANE), lambda i: (0, ridx(i), 0))],
        out_specs=([pl.BlockSpec((tm, KV_LORA), lambda i: (i, 0))] if normalize else [])
                  + [pl.BlockSpec((tm, NK), lambda i: (i, 0)), pl.BlockSpec((tm, NV), lambda i: (i, 0))],
        out_shape=([jax.ShapeDtypeStruct((R, KV_LORA), F32)] if normalize else [])
                  + [jax.ShapeDtypeStruct((R, NK), BF16), jax.ShapeDtypeStruct((R, NV), BF16)],
        compiler_params=_params(("parallel",), 8 * tm * NK * 4),
        name="kv_projection" if normalize else "kv_projection_cache",
    )(src, src, g.reshape(1, KV_LORA), wk_bf16, wv_bf16, rope_tab)


def _attn_kernel(q_ref, *refs):
    o_ref = refs[-1]
    kv = [(refs[i], refs[i + 1]) for i in range(0, len(refs) - 1, 2)]
    q = q_ref[...]
    scores = [_dot_nt(q, k_ref[...]) for k_ref, _ in kv]
    m = functools.reduce(jnp.maximum, [jnp.max(s, axis=-1, keepdims=True) for s in scores])
    probs = [jnp.exp(s - m) for s in scores]
    l = functools.reduce(jnp.add, [jnp.sum(p, axis=-1, keepdims=True) for p in probs])
    o = functools.reduce(jnp.add, [_dot(p.astype(BF16), v_ref[...]) for p, (_, v_ref) in zip(probs, kv)])
    o_ref[...] = (o / l).astype(o_ref.dtype)


def _attention(q, q_row0, n_batch, tq_total, sources):
    tq = 256
    nq = tq_total // tq
    q0 = q_row0 // tq
    in_specs = [pl.BlockSpec((tq, HEAD_PAD), lambda b, h, i: (q0 + b * nq + i, h))]
    args = [q]
    tk_total = 0
    for k, v, row0, tk in sources:
        in_specs.append(pl.BlockSpec((tk, HEAD_PAD), lambda b, h, i, o=row0 // tk: (o + b, h)))
        in_specs.append(pl.BlockSpec((tk, MLA_V), lambda b, h, i, o=row0 // tk: (o + b, h)))
        args += [k, v]
        tk_total += tk
    return pl.pallas_call(
        _attn_kernel,
        grid=(n_batch, MLA_HEADS, nq),
        in_specs=in_specs,
        out_specs=pl.BlockSpec((tq, MLA_V), lambda b, h, i: (b * nq + i, h)),
        out_shape=jax.ShapeDtypeStruct((n_batch * tq_total, MLA_HEADS * MLA_V), BF16),
        compiler_params=_params(("parallel", "parallel", "arbitrary"),
                                8 * tq * tk_total * 4 + 4 * tk_total * HEAD_PAD * 2),
        name="attention",
    )(*args)


def _fnet_kernel(z_ref, cs_ref, ct_ref, o_ref):
    yc, ys = [], []
    for g in range(FNET_GROUPS):
        zg = z_ref[:, g * FNET_GROUP_DIM:(g + 1) * FNET_GROUP_DIM].astype(BF16)
        y = _dot(zg, cs_ref[...])
        yc.append(y[:, :FNET_GROUP_DIM])
        ys.append(y[:, FNET_GROUP_DIM:])
    stacked = jnp.concatenate([jnp.concatenate(yc, axis=1), jnp.concatenate(ys, axis=1)], axis=0)
    o_ref[...] = _dot(ct_ref[...], stacked.astype(BF16)).astype(o_ref.dtype)


def _dft_tables(T):
    def cs(n):
        i = jnp.arange(n, dtype=jnp.int32)
        ang = ((i[:, None] * i[None, :]) % n).astype(F32) * (2.0 * math.pi / n)
        return jnp.cos(ang), jnp.sin(ang)
    cc, sc = cs(FNET_GROUP_DIM)
    ct, st = cs(T)
    norm = 1.0 / math.sqrt(T * FNET_GROUP_DIM)
    return (jnp.concatenate([cc, sc], axis=1).astype(BF16),
            (jnp.concatenate([ct, -st], axis=1) * norm).astype(BF16))


def _fourier_mix(hin, row0, n_batch, T):
    cs_tab, ct_tab = _dft_tables(T)
    b0 = row0 // T
    return pl.pallas_call(
        _fnet_kernel,
        grid=(n_batch,),
        in_specs=[pl.BlockSpec((T, FNET_WIDTH), lambda b: (b0 + b, COL_XF // FNET_WIDTH)),
                  pl.BlockSpec((FNET_GROUP_DIM, 2 * FNET_GROUP_DIM), lambda b: (0, 0)),
                  pl.BlockSpec((T, 2 * T), lambda b: (0, 0))],
        out_specs=pl.BlockSpec((T, FNET_WIDTH), lambda b: (b, 0)),
        out_shape=jax.ShapeDtypeStruct((n_batch * T, FNET_WIDTH), BF16),
        compiler_params=_params(("parallel",), 2 * T * 2 * T * 2 + 10 * T * FNET_WIDTH * 4),
        name="fourier_mix",
    )(hin, cs_tab, ct_tab)


def _head_sum_matrix(scale):
    i = np.arange(RWKV_WIDTH) // RWKV_HEAD_DIM
    return jnp.asarray((i[:, None] == i[None, :]).astype(np.float32) * scale, BF16)


def _rwkv_prep_kernel(x_ref, xp_ref, xn_ref, wa_ref, gl_ref, cw_ref, w0_ref, w2_ref, a0_ref, a2_ref, g2_ref,
                      kk_ref, ka_ref, rk_ref, hs_ref,
                      r_ref, kh_ref, v_ref, g_ref, bonus_ref, lw_ref, kt_ref, beta_ref,
                      *, n_ctx_tiles, tiles_per_seq):
    i = pl.program_id(0)
    j = i - n_ctx_tiles
    first = jnp.logical_or(i < n_ctx_tiles, j % tiles_per_seq == 0)
    last = jnp.logical_or(i < n_ctx_tiles, j % tiles_per_seq == tiles_per_seq - 1)
    x = x_ref[...]
    tm = x.shape[0]
    row = lax.broadcasted_iota(jnp.int32, x.shape, 0)
    prev_row = jnp.where(first, 0.0, xp_ref[7:8, :])
    next_row = jnp.where(last, 0.0, xn_ref[0:1, :])
    x_prev = jnp.where(row == 0, prev_row, pltpu.roll(x, 1, 0))
    x_next = jnp.where(row == tm - 1, next_row, pltpu.roll(x, tm - 1, 0))
    xc = x_prev * cw_ref[0:1, :] + x * cw_ref[1:2, :] + x_next * cw_ref[2:3, :]
    R_ = RWKV_WIDTH
    r, k, v = xc[:, :R_], xc[:, R_:2 * R_], xc[:, 2 * R_:]
    wl = jnp.tanh(wa_ref[:, :W_LORA]).astype(BF16)
    al = wa_ref[:, W_LORA:W_LORA + A_LORA].astype(BF16)
    g_ref[...] = _dot(_sigmoid(gl_ref[...]).astype(BF16), g2_ref[...].astype(BF16))
    kappa = k * kk_ref[...]
    ss = _dot_exact_rhs(kappa * kappa, hs_ref[...], 3)
    kh = kappa * lax.rsqrt(ss + EPS)
    kt_sum = None
    for d in range(2):
        lw_ref[d] = -DECAY_SCALE * _sigmoid(w0_ref[d:d + 1, :] + _dot(wl, w2_ref[d].astype(BF16)))
        a = _sigmoid(a0_ref[d:d + 1, :] + _dot(al, a2_ref[d].astype(BF16)))
        kt = k * (1.0 + (a - 1.0) * ka_ref[...])
        kt_ref[d] = kt
        beta_ref[d] = a * kh
        kt_sum = kt if kt_sum is None else kt_sum + kt
    r_ref[...] = r
    kh_ref[...] = kh
    v_ref[...] = v
    bonus_ref[...] = _dot_exact_rhs(r * kt_sum * rk_ref[...], hs_ref[...], 3) * v


def _rwkv_prep(hin, p, n_ctx_rows):
    R = hin.shape[0]
    tm = ROW_GROUP
    W3 = 3 * RWKV_WIDTH
    nh = tm // 8
    last_blk = R // 8 - 1
    row = lambda a: a.reshape(1, RWKV_WIDTH)
    full = lambda *s: pl.BlockSpec(s, lambda i: (0,) * len(s))
    out_rw = jax.ShapeDtypeStruct((R, RWKV_WIDTH), F32)
    out_rw2 = jax.ShapeDtypeStruct((2, R, RWKV_WIDTH), F32)
    spec_rw = pl.BlockSpec((tm, RWKV_WIDTH), lambda i: (i, 0))
    spec_rw2 = pl.BlockSpec((2, tm, RWKV_WIDTH), lambda i: (0, i, 0))
    return pl.pallas_call(
        functools.partial(_rwkv_prep_kernel, n_ctx_tiles=n_ctx_rows // tm, tiles_per_seq=DEC_SEQ // tm),
        grid=(R // tm,),
        in_specs=[pl.BlockSpec((tm, W3), lambda i: (i, COL_RKV // W3)),
                  pl.BlockSpec((8, W3), lambda i: (jnp.maximum(i * nh - 1, 0), 0)),
                  pl.BlockSpec((8, W3), lambda i: (jnp.minimum((i + 1) * nh, last_blk), 0)),
                  pl.BlockSpec((tm, LANE), lambda i: (i, COL_WA // LANE)),
                  pl.BlockSpec((tm, G_LORA), lambda i: (i, COL_GLO // G_LORA)),
                  full(3, W3), full(2, RWKV_WIDTH), full(2, W_LORA, RWKV_WIDTH),
                  full(2, RWKV_WIDTH), full(2, A_LORA, RWKV_WIDTH), full(G_LORA, RWKV_WIDTH),
                  full(1, RWKV_WIDTH), full(1, RWKV_WIDTH), full(1, RWKV_WIDTH),
                  full(RWKV_WIDTH, RWKV_WIDTH)],
        out_specs=[spec_rw] * 5 + [spec_rw2] * 3,
        out_shape=[out_rw] * 5 + [out_rw2] * 3,
        compiler_params=_params(("parallel",), 40 * tm * W3 * 4),
        name="rwkv_prep",
    )(hin, hin, hin, hin, hin, p['rwkv_conv'], p['rwkv_w0'], p['rwkv_w2'], p['rwkv_a0'], p['rwkv_a2'],
      p['rwkv_g2'], row(p['rwkv_k_k']), row(p['rwkv_k_a']), row(p['rwkv_r_k']), _head_sum_matrix(1.0))


def _rwkv_scan_kernel(rf_ref, khf_ref, vf_ref, rb_ref, khb_ref, vb_ref,
                      lwf_ref, ktf_ref, betaf_ref, lwb_ref, ktb_ref, betab_ref,
                      mi_ref, s0_ref, yf_ref, yb_ref, sout_ref, s_scr):
    c = pl.program_id(1)
    nc = pl.num_programs(1)
    N = RWKV_HEAD_DIM
    C = CHUNK
    H = RWKV_HEADS

    @pl.when(c == 0)
    def _():
        s_scr[...] = s0_ref[0]

    eye = (lax.broadcasted_iota(jnp.int32, (C, C), 0) == lax.broadcasted_iota(jnp.int32, (C, C), 1)).astype(F32)
    dirs = ((rf_ref, khf_ref, vf_ref, lwf_ref, ktf_ref, betaf_ref),
            (rb_ref, khb_ref, vb_ref, lwb_ref, ktb_ref, betab_ref))
    chains = []
    for d, (r_ref, kh_ref, v_ref, lw_ref, kt_ref, beta_ref) in enumerate(dirs):
        mi = mi_ref[d]
        lw = lw_ref[0]
        cl = _dot_exact_lhs(mi.astype(BF16), lw, 3)
        cl_tot = jnp.sum(lw, axis=0, keepdims=True)
        g_inv = jnp.exp(-cl)
        g_end = jnp.exp(cl_tot - cl)
        g_tot = jnp.exp(cl_tot)
        kh, beta, kt = kh_ref[...], beta_ref[0], kt_ref[0]
        k_dec = kh * jnp.exp(cl - lw)
        r_dec = r_ref[...] * jnp.exp(cl)
        b_inv, k_inv = beta * g_inv, kt * g_inv
        k_end, b_end = kt * g_end, beta * g_end
        v_all = v_ref[...]
        for h in range(H):
            sl = slice(h * N, (h + 1) * N)
            chains.append(dict(d=d, h=h, mi=mi, ms=mi - eye, kd=k_dec[:, sl], rd=r_dec[:, sl], bi=b_inv[:, sl],
                               ki=k_inv[:, sl], ke=k_end[:, sl], be=b_end[:, sl], v=v_all[:, sl],
                               gt=g_tot[:, sl], s0=s_scr[d, h]))

    for ch in chains:
        lhs = jnp.concatenate([ch['kd'], ch['rd']], axis=0).astype(BF16)
        rhs = jnp.concatenate([ch['bi'], ch['ki']], axis=0).astype(BF16)
        nt = _dot_nt(lhs, rhs)
        ch['a'] = nt[:C, :C] * ch['ms']
        ch['b'] = (nt[:C, C:] * ch['ms']).astype(BF16)
        ch['qp'] = (nt[C:, :] * jnp.concatenate([-ch['mi'], ch['mi']], axis=1)).astype(BF16)
        ch['xp'] = (-ch['a']).astype(BF16)
        ch['t'] = eye - ch['a']
    for _ in range(int(math.log2(C)) - 1):
        for ch in chains:
            ch['xp'] = _dot(ch['xp'], ch['xp']).astype(BF16)
        for ch in chains:
            ch['t'] = ch['t'] + _dot(ch['t'].astype(BF16), ch['xp'])
    for ch in chains:
        ch['s0b'] = ch['s0'].astype(BF16)
        ch['vb'] = ch['v'].astype(BF16)
        ch['z'] = _dot_nt(ch['kd'].astype(BF16), ch['s0b']) + _dot(ch['b'], ch['vb'])
    for ch in chains:
        u = _dot(ch['t'].astype(BF16), ch['z'].astype(BF16))
        ch['uv'] = jnp.concatenate([u.astype(BF16), ch['vb']], axis=0)
    for ch in chains:
        kb = jnp.concatenate([-ch['be'], ch['ke']], axis=0).astype(BF16)
        s_scr[ch['d'], ch['h']] = ch['s0'] * ch['gt'] + _dot_tn(ch['uv'], kb)
    for ch in chains:
        ch['y'] = _dot_nt(ch['rd'].astype(BF16), ch['s0b']) + _dot(ch['qp'], ch['uv'])
    yf_ref[...] = jnp.concatenate([ch['y'] for ch in chains[:H]], axis=1)
    yb_ref[...] = jnp.concatenate([ch['y'] for ch in chains[H:]], axis=1)

    @pl.when(c == nc - 1)
    def _():
        sout_ref[0] = s_scr[...]


def _dot_exact_lhs(a_exact_bf16, b, n):
    acc = None
    for t in _split_bf16(b, n):
        p = _dot(a_exact_bf16, t)
        acc = p if acc is None else acc + p
    return acc


def _causal_masks():
    t = np.arange(CHUNK)
    fwd = (t[None, :] <= t[:, None]).astype(np.float32)
    return jnp.asarray(np.stack([fwd, fwd.T]))


def _rwkv_scan(prep, row0, n_batch, T, s0):
    r, kh, v, _, _, lw, kt, beta = prep
    C = CHUNK
    nc = T // C
    c0 = row0 // C
    W = RWKV_WIDTH
    fwd = lambda b, c: b * nc + c
    bwd = lambda b, c: b * nc + nc - 1 - c
    shared_f = pl.BlockSpec((C, W), lambda b, c: (c0 + fwd(b, c), 0))
    shared_b = pl.BlockSpec((C, W), lambda b, c: (c0 + bwd(b, c), 0))
    dir_f = pl.BlockSpec((1, C, W), lambda b, c: (0, c0 + fwd(b, c), 0))
    dir_b = pl.BlockSpec((1, C, W), lambda b, c: (1, c0 + bwd(b, c), 0))
    st = pl.BlockSpec((1, 2, RWKV_HEADS, RWKV_HEAD_DIM, RWKV_HEAD_DIM), lambda b, c: (b, 0, 0, 0, 0))
    y_shape = jax.ShapeDtypeStruct((n_batch * T, W), F32)
    return pl.pallas_call(
        _rwkv_scan_kernel,
        grid=(n_batch, nc),
        in_specs=[shared_f] * 3 + [shared_b] * 3 + [dir_f] * 3 + [dir_b] * 3
                 + [pl.BlockSpec((2, C, C), lambda b, c: (0, 0, 0)), st],
        out_specs=[pl.BlockSpec((C, W), lambda b, c: (fwd(b, c), 0)),
                   pl.BlockSpec((C, W), lambda b, c: (bwd(b, c), 0)), st],
        out_shape=[y_shape, y_shape, jax.ShapeDtypeStruct(s0.shape, F32)],
        scratch_shapes=[pltpu.VMEM((2, RWKV_HEADS, RWKV_HEAD_DIM, RWKV_HEAD_DIM), F32)],
        compiler_params=_params(("parallel", "arbitrary"), 24 * 1024 * 1024),
        name="rwkv_scan",
    )(r, kh, v, r, kh, v, lw, kt, beta, lw, kt, beta, _causal_masks(), s0)


def _rwkv_fin_kernel(yf_ref, yb_ref, bonus_ref, g_ref, gg_ref, gb_ref, hm_ref, o_ref):
    y = yf_ref[...] + yb_ref[...]
    mu = _dot_exact_rhs(y, hm_ref[...], 3)
    yc = y - mu
    var = _dot_exact_rhs(yc * yc, hm_ref[...], 3)
    yn = yc * lax.rsqrt(var + GN_EPS) * gg_ref[...] + gb_ref[...]
    o_ref[...] = ((yn + bonus_ref[...]) * g_ref[...]).astype(o_ref.dtype)


def _rwkv_finalize(y_f, y_b, bonus, g, gn_g, gn_b):
    R = y_f.shape[0]
    tm = 512
    W = RWKV_WIDTH
    rows = pl.BlockSpec((tm, W), lambda i: (i, 0))
    vec = pl.BlockSpec((1, W), lambda i: (0, 0))
    return pl.pallas_call(
        _rwkv_fin_kernel,
        grid=(R // tm,),
        in_specs=[rows, rows, rows, rows, vec, vec, pl.BlockSpec((W, W), lambda i: (0, 0))],
        out_specs=rows,
        out_shape=jax.ShapeDtypeStruct((R, W), BF16),
        compiler_params=_params(("parallel",), 24 * tm * W * 4),
        name="rwkv_finalize",
    )(y_f, y_b, bonus, g, gn_g.reshape(1, W), gn_b.reshape(1, W), _head_sum_matrix(1.0 / RWKV_HEAD_DIM))


def _outproj_kernel(attn_ref, four_ref, rw_ref, w_ref, x_ref, g_ref, mod_ref, o_ref):
    a0 = MLA_HEADS * MLA_V
    a1 = a0 + FNET_WIDTH
    acc = _dot(attn_ref[...], w_ref[0:a0, :])
    acc += _dot(four_ref[...], w_ref[a0:a1, :])
    acc += _dot(rw_ref[...], w_ref[a1:, :])
    o_ref[...] = x_ref[...] + mod_ref[0, 2:3, :] * _rms(acc, g_ref[...])


def _out_projection(attn, four, rw, w_bf16, x, g, modg):
    R, D = x.shape
    tm = 256
    K = w_bf16.shape[0]
    return pl.pallas_call(
        _outproj_kernel,
        grid=(R // tm,),
        in_specs=[pl.BlockSpec((tm, attn.shape[1]), lambda i: (i, 0)),
                  pl.BlockSpec((tm, four.shape[1]), lambda i: (i, 0)),
                  pl.BlockSpec((tm, rw.shape[1]), lambda i: (i, 0)),
                  pl.BlockSpec((K, D), lambda i: (0, 0)),
                  pl.BlockSpec((tm, D), lambda i: (i, 0)),
                  pl.BlockSpec((1, D), lambda i: (0, 0)),
                  pl.BlockSpec((1, 6, D), lambda i: (i * (tm // ROW_GROUP), 0, 0))],
        out_specs=pl.BlockSpec((tm, D), lambda i: (i, 0)),
        out_shape=jax.ShapeDtypeStruct((R, D), F32),
        compiler_params=_params(("parallel",), 2 * K * D * 2 + 8 * tm * D * 4),
        name="out_projection",
    )(attn, four, rw, w_bf16, x, g.reshape(1, D), modg)


def _ffn_up_kernel(x_ref, g_ref, mod_ref, wg_ref, wv_ref, cwg_ref, cwv_ref, cbg_ref, cbv_ref, o_ref, a_scr,
                   *, n_ctx_tiles):
    @pl.when(pl.program_id(1) == 0)
    def _():
        h = _rms(x_ref[...], g_ref[...]) * (1.0 + mod_ref[0, 4:5, :]) + mod_ref[0, 3:4, :]
        a_scr[...] = h.astype(BF16)

    tm = a_scr.shape[0]
    seq = jnp.where(pl.program_id(0) < n_ctx_tiles, SEQ, DEC_SEQ)
    a = a_scr[...]

    def conv(u, cw_ref, cb_ref):
        pos = lax.broadcasted_iota(jnp.int32, u.shape, 0) & (seq - 1)
        u_prev = jnp.where(pos == 0, 0.0, pltpu.roll(u, 1, 0))
        u_next = jnp.where(pos == seq - 1, 0.0, pltpu.roll(u, tm - 1, 0))
        return u_prev * cw_ref[0:1, :] + u * cw_ref[1:2, :] + u_next * cw_ref[2:3, :] + cb_ref[...]

    gate = conv(_dot(a, wg_ref[...]), cwg_ref, cbg_ref)
    val = conv(_dot(a, wv_ref[...]), cwv_ref, cbv_ref)
    o_ref[...] = (gate * _sigmoid(gate) * val).astype(o_ref.dtype)


def _ffn_up(x, g, modg, w_bf16, conv_w, conv_b, n_ctx_rows):
    R, D = x.shape
    tm, tn = DEC_SEQ, 512
    nj = D_FF // tn
    cb = conv_b.reshape(1, 2 * D_FF)
    return pl.pallas_call(
        functools.partial(_ffn_up_kernel, n_ctx_tiles=n_ctx_rows // tm),
        grid=(R // tm, nj),
        in_specs=[pl.BlockSpec((tm, D), lambda i, j: (i, 0)),
                  pl.BlockSpec((1, D), lambda i, j: (0, 0)),
                  pl.BlockSpec((1, 6, D), lambda i, j: (i * (tm // ROW_GROUP), 0, 0)),
                  pl.BlockSpec((D, tn), lambda i, j: (0, j)),
                  pl.BlockSpec((D, tn), lambda i, j: (0, j + nj)),
                  pl.BlockSpec((3, tn), lambda i, j: (0, j)),
                  pl.BlockSpec((3, tn), lambda i, j: (0, j + nj)),
                  pl.BlockSpec((1, tn), lambda i, j: (0, j)),
                  pl.BlockSpec((1, tn), lambda i, j: (0, j + nj))],
        out_specs=pl.BlockSpec((tm, tn), lambda i, j: (i, j)),
        out_shape=jax.ShapeDtypeStruct((R, D_FF), BF16),
        scratch_shapes=[pltpu.VMEM((tm, D), BF16)],
        compiler_params=_params(("parallel", "arbitrary"),
                                2 * tm * D * 4 + tm * D * 2 + 4 * D * tn * 2 + 12 * tm * tn * 4),
        name="ffn_up",
    )(x, g.reshape(1, D), modg, w_bf16, w_bf16, conv_w, conv_w, cb, cb)


def _ffn_down_kernel(a_ref, w_ref, x_ref, g_ref, mod_ref, o_ref, acc_scr):
    k = pl.program_id(1)

    @pl.when(k == 0)
    def _():
        acc_scr[...] = jnp.zeros_like(acc_scr)

    acc_scr[...] += _dot(a_ref[...], w_ref[...])

    @pl.when(k == pl.num_programs(1) - 1)
    def _():
        o_ref[...] = x_ref[...] + mod_ref[0, 5:6, :] * _rms(acc_scr[...], g_ref[...])


def _ffn_down(act, w_bf16, x, g, modg):
    R, D = x.shape
    K = act.shape[1]
    tm, tk = 1024, 512
    return pl.pallas_call(
        _ffn_down_kernel,
        grid=(R // tm, K // tk),
        in_specs=[pl.BlockSpec((tm, tk), lambda i, k: (i, k)),
                  pl.BlockSpec((tk, D), lambda i, k: (k, 0)),
                  pl.BlockSpec((tm, D), lambda i, k: (i, 0)),
                  pl.BlockSpec((1, D), lambda i, k: (0, 0)),
                  pl.BlockSpec((1, 6, D), lambda i, k: (i * (tm // ROW_GROUP), 0, 0))],
        out_specs=pl.BlockSpec((tm, D), lambda i, k: (i, 0)),
        out_shape=jax.ShapeDtypeStruct((R, D), F32),
        scratch_shapes=[pltpu.VMEM((tm, D), F32)],
        compiler_params=_params(("parallel", "arbitrary"), 6 * tm * D * 4 + 2 * tk * D * 2 + 2 * tm * tk * 2),
        name="ffn_down",
    )(act, w_bf16, x, g.reshape(1, D), modg)


def _layout_w_in(w):
    o = np.cumsum([0, Q_LORA, KV_LORA, MLA_ROPE, FNET_WIDTH, 3 * RWKV_WIDTH, W_LORA, A_LORA, G_LORA])
    q_dn, kv_dn, k_rope, xf, rkv, w_lo, a_lo, g_lo = [w[:, o[i]:o[i + 1]] for i in range(8)]
    zeros = jnp.zeros((w.shape[0], LANE - MLA_ROPE), w.dtype)
    return jnp.concatenate([rkv, q_dn, xf, kv_dn, k_rope, zeros, w_lo, a_lo, g_lo], axis=1).astype(BF16)


def _layout_w_uq(w):
    w = w.reshape(Q_LORA, MLA_HEADS, MLA_NOPE + MLA_ROPE)
    w = jnp.pad(w, ((0, 0), (0, 0), (0, HEAD_PAD - MLA_NOPE - MLA_ROPE)))
    return w.reshape(Q_LORA, MLA_HEADS * HEAD_PAD).astype(BF16)


def _layout_w_ukv(w):
    w = w.reshape(KV_LORA, MLA_HEADS, MLA_NOPE + MLA_V)
    wk = w[:, :, :MLA_NOPE].reshape(KV_LORA, MLA_HEADS * MLA_NOPE)
    wv = w[:, :, MLA_NOPE:].reshape(KV_LORA, MLA_HEADS * MLA_V)
    return wk.astype(BF16), wv.astype(BF16)


def _rope_table(tm):
    t = jnp.arange(DEC_SEQ)
    rowp = (t // GRID_W).astype(F32)
    colp = (t % GRID_W).astype(F32)
    n = MLA_ROPE // 4
    inv = ROPE_BASE ** (-jnp.arange(n, dtype=F32) / n)
    ang = jnp.concatenate([rowp[:, None] * inv, colp[:, None] * inv], axis=-1)
    cos = jnp.concatenate([jnp.ones((tm, MLA_ROPE // 2), F32), jnp.cos(ang)], axis=0)
    sin = jnp.concatenate([jnp.zeros((tm, MLA_ROPE // 2), F32), jnp.sin(ang)], axis=0)
    z32 = jnp.zeros_like(cos)
    z64 = jnp.zeros((cos.shape[0], LANE - MLA_ROPE), F32)
    mul = jnp.concatenate([cos, cos, z64], axis=1)
    left = jnp.concatenate([-sin, z32, z64], axis=1)
    right = jnp.concatenate([z32, sin, z64], axis=1)
    return jnp.stack([mul, left, right])


def kernel(x_prompt, x_sample, cache_mla_ckv, cache_mla_krope, state_rwkv, c, c_ctx, w_mod, b_mod, g_pre_mix, g_post_mix, g_pre_ffn, g_post_ffn, w_in, g_q_norm, w_uq, g_kv_norm, w_ukv, rwkv_conv, rwkv_w0, rwkv_w2, rwkv_a0, rwkv_a2, rwkv_g2, rwkv_k_k, rwkv_k_a, rwkv_r_k, rwkv_gn_g, rwkv_gn_b, w_out, ffn_w_up, ffn_conv, ffn_conv_b, ffn_w_down):
    Bc, Tc, D = x_prompt.shape
    Bl, Tl, _ = x_sample.shape
    L = w_mod.shape[0]
    assert (Tc, Tl, D) == (SEQ, DEC_SEQ, D_MODEL) and cache_mla_ckv.shape[2] == PAST_LEN
    assert Bl + 1 <= 8 and (Bc * Tc) % DEC_SEQ == 0
    n_ctx = Bc * Tc
    n_lat = Bl * Tl
    x = jnp.concatenate([x_prompt.reshape(n_ctx, D), x_sample.reshape(n_lat, D)], axis=0)

    cond = jnp.zeros((8, D), F32).at[0].set(c_ctx).at[1:1 + Bl].set(c)
    mods = _modulation(cond, w_mod, b_mod)
    group_row = np.concatenate([np.zeros(n_ctx // ROW_GROUP, np.int32),
                                1 + np.arange(n_lat // ROW_GROUP, dtype=np.int32) // (Tl // ROW_GROUP)])

    tm_mla = 512
    rope_tab = _rope_table(tm_mla)
    ridx = _rope_index(tm_mla, n_ctx)
    ridx_cache = lambda i: 0
    zero_state = jnp.zeros((Bc, 2, RWKV_HEADS, RWKV_HEAD_DIM, RWKV_HEAD_DIM), F32)

    ckv_out, krope_out, state_out = [], [], []
    for l in range(L):
        modg = mods[l][group_row].reshape(-1, 6, D)
        wk, wv = _layout_w_ukv(w_ukv[l])
        p = dict(rwkv_conv=rwkv_conv[l], rwkv_w0=rwkv_w0[l], rwkv_w2=rwkv_w2[l], rwkv_a0=rwkv_a0[l],
                 rwkv_a2=rwkv_a2[l], rwkv_g2=rwkv_g2[l], rwkv_k_k=rwkv_k_k[l], rwkv_k_a=rwkv_k_a[l],
                 rwkv_r_k=rwkv_r_k[l])

        hin = _in_projection(x, g_pre_mix[l], modg, _layout_w_in(w_in[l]))

        q = _q_projection(hin, g_q_norm[l], _layout_w_uq(w_uq[l]), rope_tab, n_ctx)
        ckv, k, v = _kv_projection(hin, COL_KVDN // KV_LORA, COL_KROPE // LANE, g_kv_norm[l], wk, wv,
                                   rope_tab, ridx, tm_mla, True)
        cache_src = jnp.concatenate(
            [cache_mla_ckv[:, l].reshape(Bl * PAST_LEN, KV_LORA),
             jnp.pad(cache_mla_krope[:, l].reshape(Bl * PAST_LEN, MLA_ROPE), ((0, 0), (0, LANE - MLA_ROPE)))],
            axis=1)
        k_c, v_c = _kv_projection(cache_src, 0, KV_LORA // LANE, g_kv_norm[l], wk, wv,
                                     rope_tab, ridx_cache, PAST_LEN, False)
        attn_ctx = _attention(q, 0, Bc, Tc, [(k, v, 0, Tc)])
        attn_lat = _attention(q, n_ctx, Bl, Tl, [(k_c, v_c, 0, PAST_LEN), (k, v, n_ctx, Tl)])
        attn = jnp.concatenate([attn_ctx, attn_lat], axis=0)

        four = jnp.concatenate([_fourier_mix(hin, 0, Bc, Tc), _fourier_mix(hin, n_ctx, Bl, Tl)], axis=0)

        prep = _rwkv_prep(hin, p, n_ctx)
        yf_ctx, yb_ctx, s_ctx = _rwkv_scan(prep, 0, Bc, Tc, zero_state)
        yf_lat, yb_lat, _ = _rwkv_scan(prep, n_ctx, Bl, Tl, state_rwkv[:, l])
        rw = _rwkv_finalize(jnp.concatenate([yf_ctx, yf_lat], axis=0), jnp.concatenate([yb_ctx, yb_lat], axis=0),
                            prep[4], prep[3], rwkv_gn_g[l], rwkv_gn_b[l])

        x = _out_projection(attn, four, rw, w_out[l].astype(BF16), x, g_post_mix[l], modg)
        act = _ffn_up(x, g_pre_ffn[l], modg, ffn_w_up[l].astype(BF16), ffn_conv[l], ffn_conv_b[l], n_ctx)
        x = _ffn_down(act, ffn_w_down[l].astype(BF16), x, g_post_ffn[l], modg)

        ckv_out.append(ckv[:n_ctx].reshape(Bc, Tc, KV_LORA))
        krope_out.append(hin[:n_ctx, COL_KROPE:COL_KROPE + MLA_ROPE].reshape(Bc, Tc, MLA_ROPE))
        state_out.append(s_ctx)

    return (x[:n_ctx].reshape(Bc, Tc, D), x[n_ctx:].reshape(Bl, Tl, D),
            jnp.stack(ckv_out, axis=1), jnp.stack(krope_out, axis=1), jnp.stack(state_out, axis=1))
```

```python
import functools
import math

import jax
import jax.numpy as jnp
import numpy as np
from jax import lax
from jax.experimental import pallas as pl
from jax.experimental.pallas import tpu as pltpu

F32 = jnp.float32
BF16 = jnp.bfloat16

D_MODEL = 2048
SEQ = 256
DEC_SEQ = 1024
PAST_LEN = 256
GRID_W = 64
MLA_HEADS = 8
MLA_NOPE = 128
MLA_ROPE = 64
MLA_V = 128
Q_LORA = 512
KV_LORA = 256
FNET_GROUPS = 4
FNET_GROUP_DIM = 128
FNET_WIDTH = FNET_GROUPS * FNET_GROUP_DIM
RWKV_HEADS = 8
RWKV_HEAD_DIM = 64
RWKV_WIDTH = RWKV_HEADS * RWKV_HEAD_DIM
W_LORA = 64
A_LORA = 64
G_LORA = 128
D_FF = 5632
ROPE_BASE = 10000.0
EPS = 1e-6
GN_EPS = 64e-5
DECAY_SCALE = math.exp(-0.5)

LANE = 128
ROW_GROUP = 256
HEAD_PAD = 256
CHUNK = 64
VMEM_CAP = 60 * 1024 * 1024

COL_RKV = 0
COL_QDN = 3 * RWKV_WIDTH
COL_XF = COL_QDN + Q_LORA
COL_KVDN = COL_XF + FNET_WIDTH
COL_KROPE = COL_KVDN + KV_LORA
COL_WA = COL_KROPE + LANE
COL_GLO = COL_WA + LANE
IN_PAD = COL_GLO + G_LORA


SMALL_OPERANDS = 8 * 1024 * 1024


def _params(sem, nbytes):
    return pltpu.CompilerParams(dimension_semantics=sem,
                                vmem_limit_bytes=int(min(VMEM_CAP, nbytes + SMALL_OPERANDS)))


def _dot(a, b):
    return jnp.dot(a, b, preferred_element_type=F32)


def _dot_nt(a, b):
    return lax.dot_general(a, b, (((1,), (1,)), ((), ())), preferred_element_type=F32)


def _dot_tn(a, b):
    return lax.dot_general(a, b, (((0,), (0,)), ((), ())), preferred_element_type=F32)


def _split_bf16(x, n):
    terms = []
    rem = x
    for i in range(n):
        t = rem.astype(BF16)
        terms.append(t)
        if i + 1 < n:
            rem = rem - t.astype(F32)
    return terms


def _dot_exact_rhs(a, b_exact_bf16, n):
    acc = None
    for t in _split_bf16(a, n):
        p = _dot(t, b_exact_bf16)
        acc = p if acc is None else acc + p
    return acc


def _rms(x, g):
    return x * lax.rsqrt(jnp.mean(x * x, axis=-1, keepdims=True) + EPS) * g


def _sigmoid(x):
    return 1.0 / (1.0 + jnp.exp(-x))


def _mod_kernel(cond_ref, w_ref, b_ref, o_ref):
    c = cond_ref[...]
    s = (c * _sigmoid(c)).astype(BF16)
    o_ref[0] = _dot(s, w_ref[0].astype(BF16)) + b_ref[0]


def _modulation(cond, w_mod, b_mod):
    L, D, N = w_mod.shape
    tn = 1024
    return pl.pallas_call(
        _mod_kernel,
        grid=(L, N // tn),
        in_specs=[pl.BlockSpec((8, D), lambda l, j: (0, 0)),
                  pl.BlockSpec((1, D, tn), lambda l, j: (l, 0, j)),
                  pl.BlockSpec((1, 1, tn), lambda l, j: (l, 0, j))],
        out_specs=pl.BlockSpec((1, 8, tn), lambda l, j: (l, 0, j)),
        out_shape=jax.ShapeDtypeStruct((L, 8, N), F32),
        compiler_params=_params(("parallel", "parallel"), 3 * D * tn * 4),
        name="modulation",
    )(cond, w_mod, b_mod.reshape(L, 1, N))


def _inproj_kernel(x_ref, g_ref, mod_ref, w_ref, o_ref, a_scr):
    @pl.when(pl.program_id(1) == 0)
    def _():
        h = _rms(x_ref[...], g_ref[...]) * (1.0 + mod_ref[0, 1:2, :]) + mod_ref[0, 0:1, :]
        a_scr[...] = h.astype(BF16)

    o_ref[...] = _dot(a_scr[...], w_ref[...])


def _in_projection(x, g, modg, w_bf16, l):
    R, D = x.shape
    N = w_bf16.shape[2]
    tm, tn = 1024, 640
    return pl.pallas_call(
        _inproj_kernel,
        grid=(R // tm, N // tn),
        in_specs=[pl.BlockSpec((tm, D), lambda i, j: (i, 0)),
                  pl.BlockSpec((1, D), lambda i, j: (0, 0)),
                  pl.BlockSpec((1, 6, D), lambda i, j: (i * (tm // ROW_GROUP), 0, 0)),
                  pl.BlockSpec((None, D, tn), lambda i, j: (l, 0, j))],
        out_specs=pl.BlockSpec((tm, tn), lambda i, j: (i, j)),
        out_shape=jax.ShapeDtypeStruct((R, N), F32),
        scratch_shapes=[pltpu.VMEM((tm, D), BF16)],
        compiler_params=_params(("parallel", "arbitrary"),
                                2 * tm * D * 4 + tm * D * 2 + 2 * D * tn * 2 + 4 * tm * tn * 4),
        name="in_projection",
    )(x, g.reshape(1, D), modg, w_bf16)


def _rope_block(blk, rope_ref):
    return (blk * rope_ref[0]
            + pltpu.roll(blk, LANE - MLA_ROPE // 2, 1) * rope_ref[1]
            + pltpu.roll(blk, MLA_ROPE // 2, 1) * rope_ref[2])


def _q_kernel(qdn_ref, g_ref, w_ref, rope_ref, o_ref):
    scale = (MLA_NOPE + MLA_ROPE) ** -0.5
    qn = _rms(qdn_ref[...], g_ref[...]).astype(BF16)
    q = _dot(qn, w_ref[...]) * scale
    parts = []
    for h in range(MLA_HEADS):
        base = h * HEAD_PAD
        parts.append(q[:, base:base + MLA_NOPE])
        parts.append(_rope_block(q[:, base + MLA_NOPE:base + HEAD_PAD], rope_ref))
    o_ref[...] = jnp.concatenate(parts, axis=1).astype(o_ref.dtype)


def _rope_index(tm, n_ctx_rows):
    n_ctx_tiles = n_ctx_rows // tm
    per_seq = DEC_SEQ // tm

    def index(i):
        return jnp.where(i < n_ctx_tiles, 0, 1 + (i - n_ctx_tiles) % per_seq)
    return index


def _q_projection(hin, g, w_bf16, l, rope_tab, n_ctx_rows):
    R = hin.shape[0]
    tm = 512
    N = MLA_HEADS * HEAD_PAD
    ridx = _rope_index(tm, n_ctx_rows)
    return pl.pallas_call(
        _q_kernel,
        grid=(R // tm,),
        in_specs=[pl.BlockSpec((tm, Q_LORA), lambda i: (i, COL_QDN // Q_LORA)),
                  pl.BlockSpec((1, Q_LORA), lambda i: (0, 0)),
                  pl.BlockSpec((None, Q_LORA, N), lambda i: (l, 0, 0)),
                  pl.BlockSpec((3, tm, LANE), lambda i: (0, ridx(i), 0))],
        out_specs=pl.BlockSpec((tm, N), lambda i: (i, 0)),
        out_shape=jax.ShapeDtypeStruct((R, N), BF16),
        compiler_params=_params(("parallel",), 2 * Q_LORA * N * 2 + 6 * tm * N * 4),
        name="q_projection",
    )(hin, g.reshape(1, Q_LORA), w_bf16, rope_tab)


def _kv_kernel(kvdn_ref, kr_ref, g_ref, wk_ref, wv_ref, rope_ref, *out_refs, normalize):
    k_ref, v_ref = out_refs[-2:]
    c = kvdn_ref[...]
    if normalize:
        c = _rms(c, g_ref[...])
        out_refs[0][...] = c
    cb = c.astype(BF16)
    kn = _dot(cb, wk_ref[...])
    v_ref[...] = _dot(cb, wv_ref[...]).astype(v_ref.dtype)
    kr = _rope_block(kr_ref[...], rope_ref)
    parts = []
    for h in range(MLA_HEADS):
        parts.append(kn[:, h * MLA_NOPE:(h + 1) * MLA_NOPE])
        parts.append(kr)
    k_ref[...] = jnp.concatenate(parts, axis=1).astype(k_ref.dtype)


def _kv_projection(src, kv_col, kr_col, g, wk_bf16, wv_bf16, l, rope_tab, ridx, tm, normalize):
    R = src.shape[0]
    NK = MLA_HEADS * HEAD_PAD
    NV = MLA_HEADS * MLA_V
    return pl.pallas_call(
        functools.partial(_kv_kernel, normalize=normalize),
        grid=(R // tm,),
        in_specs=[pl.BlockSpec((tm, KV_LORA), lambda i: (i, kv_col)),
                  pl.BlockSpec((tm, LANE), lambda i: (i, kr_col)),
                  pl.BlockSpec((1, KV_LORA), lambda i: (0, 0)),
                  pl.BlockSpec((None, KV_LORA, MLA_HEADS * MLA_NOPE), lambda i: (l, 0, 0)),
                  pl.BlockSpec((None, KV_LORA, NV), lambda i: (l, 0, 0)),
                  pl.BlockSpec((3, tm, LANE), lambda i: (0, ridx(i), 0))],
        out_specs=([pl.BlockSpec((tm, KV_LORA), lambda i: (i, 0))] if normalize else [])
                  + [pl.BlockSpec((tm, NK), lambda i: (i, 0)), pl.BlockSpec((tm, NV), lambda i: (i, 0))],
        out_shape=([jax.ShapeDtypeStruct((R, KV_LORA), F32)] if normalize else [])
                  + [jax.ShapeDtypeStruct((R, NK), BF16), jax.ShapeDtypeStruct((R, NV), BF16)],
        compiler_params=_params(("parallel",), 8 * tm * NK * 4),
        name="kv_projection" if normalize else "kv_projection_cache",
    )(src, src, g.reshape(1, KV_LORA), wk_bf16, wv_bf16, rope_tab)


def _attn_kernel(q_ref, *refs):
    o_ref = refs[-1]
    kv = [(refs[i], refs[i + 1]) for i in range(0, len(refs) - 1, 2)]
    for h in range(MLA_HEADS):
        qk = slice(h * HEAD_PAD, (h + 1) * HEAD_PAD)
        vo = slice(h * MLA_V, (h + 1) * MLA_V)
        q = q_ref[:, qk]
        scores = [_dot_nt(q, k_ref[:, qk]) for k_ref, _ in kv]
        m = functools.reduce(jnp.maximum, [jnp.max(s, axis=-1, keepdims=True) for s in scores])
        probs = [jnp.exp(s - m) for s in scores]
        l = functools.reduce(jnp.add, [jnp.sum(p, axis=-1, keepdims=True) for p in probs])
        o = functools.reduce(jnp.add, [_dot(p.astype(BF16), v_ref[:, vo]) for p, (_, v_ref) in zip(probs, kv)])
        o_ref[:, vo] = (o / l).astype(o_ref.dtype)


def _attention(q, q_row0, n_batch, tq_total, sources):
    tq = min(512, tq_total)
    nq = tq_total // tq
    q0 = q_row0 // tq
    NQ = MLA_HEADS * HEAD_PAD
    NV = MLA_HEADS * MLA_V
    in_specs = [pl.BlockSpec((tq, NQ), lambda b, i: (q0 + b * nq + i, 0))]
    args = [q]
    tk_total = 0
    for k, v, row0, tk in sources:
        in_specs.append(pl.BlockSpec((tk, NQ), lambda b, i, o=row0 // tk: (o + b, 0)))
        in_specs.append(pl.BlockSpec((tk, NV), lambda b, i, o=row0 // tk: (o + b, 0)))
        args += [k, v]
        tk_total += tk
    return pl.pallas_call(
        _attn_kernel,
        grid=(n_batch, nq),
        in_specs=in_specs,
        out_specs=pl.BlockSpec((tq, NV), lambda b, i: (b * nq + i, 0)),
        out_shape=jax.ShapeDtypeStruct((n_batch * tq_total, NV), BF16),
        compiler_params=_params(("parallel", "arbitrary"),
                                2 * (tq + tk_total) * (NQ + NV) * 2 + 6 * tq * tk_total * 4),
        name="attention",
    )(*args)


def _fnet_kernel(z_ref, cs_ref, ct_ref, o_ref):
    yc, ys = [], []
    for g in range(FNET_GROUPS):
        zg = z_ref[:, g * FNET_GROUP_DIM:(g + 1) * FNET_GROUP_DIM].astype(BF16)
        y = _dot(zg, cs_ref[...])
        yc.append(y[:, :FNET_GROUP_DIM])
        ys.append(y[:, FNET_GROUP_DIM:])
    stacked = jnp.concatenate([jnp.concatenate(yc, axis=1), jnp.concatenate(ys, axis=1)], axis=0)
    o_ref[...] = _dot(ct_ref[...], stacked.astype(BF16)).astype(o_ref.dtype)


def _dft_tables(T):
    def cs(n):
        i = jnp.arange(n, dtype=jnp.int32)
        ang = ((i[:, None] * i[None, :]) % n).astype(F32) * (2.0 * math.pi / n)
        return jnp.cos(ang), jnp.sin(ang)
    cc, sc = cs(FNET_GROUP_DIM)
    ct, st = cs(T)
    norm = 1.0 / math.sqrt(T * FNET_GROUP_DIM)
    return (jnp.concatenate([cc, sc], axis=1).astype(BF16),
            (jnp.concatenate([ct, -st], axis=1) * norm).astype(BF16))


def _fourier_mix(hin, row0, n_batch, T):
    cs_tab, ct_tab = _dft_tables(T)
    b0 = row0 // T
    return pl.pallas_call(
        _fnet_kernel,
        grid=(n_batch,),
        in_specs=[pl.BlockSpec((T, FNET_WIDTH), lambda b: (b0 + b, COL_XF // FNET_WIDTH)),
                  pl.BlockSpec((FNET_GROUP_DIM, 2 * FNET_GROUP_DIM), lambda b: (0, 0)),
                  pl.BlockSpec((T, 2 * T), lambda b: (0, 0))],
        out_specs=pl.BlockSpec((T, FNET_WIDTH), lambda b: (b, 0)),
        out_shape=jax.ShapeDtypeStruct((n_batch * T, FNET_WIDTH), BF16),
        compiler_params=_params(("parallel",), 2 * T * 2 * T * 2 + 10 * T * FNET_WIDTH * 4),
        name="fourier_mix",
    )(hin, cs_tab, ct_tab)


def _head_sum_matrix(scale):
    i = np.arange(RWKV_WIDTH) // RWKV_HEAD_DIM
    return jnp.asarray((i[:, None] == i[None, :]).astype(np.float32) * scale, BF16)


def _rwkv_prep_kernel(x_ref, xp_ref, xn_ref, wa_ref, gl_ref, cw_ref, w0_ref, w2_ref, a0_ref, a2_ref, g2_ref,
                      kk_ref, ka_ref, rk_ref, hs_ref,
                      r_ref, kh_ref, v_ref, g_ref, bonus_ref, lw_ref, kt_ref, beta_ref,
                      *, n_ctx_tiles, tiles_per_seq):
    i = pl.program_id(0)
    j = i - n_ctx_tiles
    first = jnp.logical_or(i < n_ctx_tiles, j % tiles_per_seq == 0)
    last = jnp.logical_or(i < n_ctx_tiles, j % tiles_per_seq == tiles_per_seq - 1)
    x = x_ref[...]
    tm = x.shape[0]
    row = lax.broadcasted_iota(jnp.int32, x.shape, 0)
    prev_row = jnp.where(first, 0.0, xp_ref[7:8, :])
    next_row = jnp.where(last, 0.0, xn_ref[0:1, :])
    x_prev = jnp.where(row == 0, prev_row, pltpu.roll(x, 1, 0))
    x_next = jnp.where(row == tm - 1, next_row, pltpu.roll(x, tm - 1, 0))
    xc = x_prev * cw_ref[0:1, :] + x * cw_ref[1:2, :] + x_next * cw_ref[2:3, :]
    R_ = RWKV_WIDTH
    r, k, v = xc[:, :R_], xc[:, R_:2 * R_], xc[:, 2 * R_:]
    wl = jnp.tanh(wa_ref[:, :W_LORA]).astype(BF16)
    al = wa_ref[:, W_LORA:W_LORA + A_LORA].astype(BF16)
    g_ref[...] = _dot(_sigmoid(gl_ref[...]).astype(BF16), g2_ref[...].astype(BF16))
    kappa = k * kk_ref[...]
    ss = _dot_exact_rhs(kappa * kappa, hs_ref[...], 3)
    kh = kappa * lax.rsqrt(ss + EPS)
    kt_sum = None
    for d in range(2):
        lw_ref[d] = -DECAY_SCALE * _sigmoid(w0_ref[d:d + 1, :] + _dot(wl, w2_ref[d].astype(BF16)))
        a = _sigmoid(a0_ref[d:d + 1, :] + _dot(al, a2_ref[d].astype(BF16)))
        kt = k * (1.0 + (a - 1.0) * ka_ref[...])
        kt_ref[d] = kt
        beta_ref[d] = a * kh
        kt_sum = kt if kt_sum is None else kt_sum + kt
    r_ref[...] = r
    kh_ref[...] = kh
    v_ref[...] = v
    bonus_ref[...] = _dot_exact_rhs(r * kt_sum * rk_ref[...], hs_ref[...], 3) * v


def _rwkv_prep(hin, p, n_ctx_rows):
    R = hin.shape[0]
    tm = ROW_GROUP
    W3 = 3 * RWKV_WIDTH
    nh = tm // 8
    last_blk = R // 8 - 1
    row = lambda a: a.reshape(1, RWKV_WIDTH)
    full = lambda *s: pl.BlockSpec(s, lambda i: (0,) * len(s))
    out_rw = jax.ShapeDtypeStruct((R, RWKV_WIDTH), F32)
    out_rw2 = jax.ShapeDtypeStruct((2, R, RWKV_WIDTH), F32)
    spec_rw = pl.BlockSpec((tm, RWKV_WIDTH), lambda i: (i, 0))
    spec_rw2 = pl.BlockSpec((2, tm, RWKV_WIDTH), lambda i: (0, i, 0))
    return pl.pallas_call(
        functools.partial(_rwkv_prep_kernel, n_ctx_tiles=n_ctx_rows // tm, tiles_per_seq=DEC_SEQ // tm),
        grid=(R // tm,),
        in_specs=[pl.BlockSpec((tm, W3), lambda i: (i, COL_RKV // W3)),
                  pl.BlockSpec((8, W3), lambda i: (jnp.maximum(i * nh - 1, 0), 0)),
                  pl.BlockSpec((8, W3), lambda i: (jnp.minimum((i + 1) * nh, last_blk), 0)),
                  pl.BlockSpec((tm, LANE), lambda i: (i, COL_WA // LANE)),
                  pl.BlockSpec((tm, G_LORA), lambda i: (i, COL_GLO // G_LORA)),
                  full(3, W3), full(2, RWKV_WIDTH), full(2, W_LORA, RWKV_WIDTH),
                  full(2, RWKV_WIDTH), full(2, A_LORA, RWKV_WIDTH), full(G_LORA, RWKV_WIDTH),
                  full(1, RWKV_WIDTH), full(1, RWKV_WIDTH), full(1, RWKV_WIDTH),
                  full(RWKV_WIDTH, RWKV_WIDTH)],
        out_specs=[spec_rw] * 5 + [spec_rw2] * 3,
        out_shape=[out_rw] * 5 + [out_rw2] * 3,
        compiler_params=_params(("parallel",), 40 * tm * W3 * 4),
        name="rwkv_prep",
    )(hin, hin, hin, hin, hin, p['rwkv_conv'], p['rwkv_w0'], p['rwkv_w2'], p['rwkv_a0'], p['rwkv_a2'],
      p['rwkv_g2'], row(p['rwkv_k_k']), row(p['rwkv_k_a']), row(p['rwkv_r_k']), _head_sum_matrix(1.0))


def _rwkv_scan_kernel(rf_ref, khf_ref, vf_ref, rb_ref, khb_ref, vb_ref,
                      lwf_ref, ktf_ref, betaf_ref, lwb_ref, ktb_ref, betab_ref,
                      mi_ref, s0_ref, yf_ref, yb_ref, sout_ref, s_scr):
    c = pl.program_id(1)
    nc = pl.num_programs(1)
    N = RWKV_HEAD_DIM
    C = CHUNK
    H = RWKV_HEADS

    @pl.when(c == 0)
    def _():
        s_scr[...] = s0_ref[0]

    eye = (lax.broadcasted_iota(jnp.int32, (C, C), 0) == lax.broadcasted_iota(jnp.int32, (C, C), 1)).astype(F32)
    dirs = ((rf_ref, khf_ref, vf_ref, lwf_ref, ktf_ref, betaf_ref),
            (rb_ref, khb_ref, vb_ref, lwb_ref, ktb_ref, betab_ref))
    chains = []
    for d, (r_ref, kh_ref, v_ref, lw_ref, kt_ref, beta_ref) in enumerate(dirs):
        mi = mi_ref[d]
        lw = lw_ref[0]
        cl = _dot_exact_lhs(mi.astype(BF16), lw, 3)
        cl_tot = jnp.sum(lw, axis=0, keepdims=True)
        g_inv = jnp.exp(-cl)
        g_end = jnp.exp(cl_tot - cl)
        g_tot = jnp.exp(cl_tot)
        kh, beta, kt = kh_ref[...], beta_ref[0], kt_ref[0]
        k_dec = kh * jnp.exp(cl - lw)
        r_dec = r_ref[...] * jnp.exp(cl)
        b_inv, k_inv = beta * g_inv, kt * g_inv
        k_end, b_end = kt * g_end, beta * g_end
        v_all = v_ref[...]
        for h in range(H):
            sl = slice(h * N, (h + 1) * N)
            chains.append(dict(d=d, h=h, mi=mi, ms=mi - eye, kd=k_dec[:, sl], rd=r_dec[:, sl], bi=b_inv[:, sl],
                               ki=k_inv[:, sl], ke=k_end[:, sl], be=b_end[:, sl], v=v_all[:, sl],
                               gt=g_tot[:, sl], s0=s_scr[d, h]))

    for ch in chains:
        lhs = jnp.concatenate([ch['kd'], ch['rd']], axis=0).astype(BF16)
        rhs = jnp.concatenate([ch['bi'], ch['ki']], axis=0).astype(BF16)
        nt = _dot_nt(lhs, rhs)
        ch['a'] = nt[:C, :C] * ch['ms']
        ch['b'] = (nt[:C, C:] * ch['ms']).astype(BF16)
        ch['qp'] = (nt[C:, :] * jnp.concatenate([-ch['mi'], ch['mi']], axis=1)).astype(BF16)
        ch['xp'] = (-ch['a']).astype(BF16)
        ch['t'] = eye - ch['a']
    for _ in range(int(math.log2(C)) - 1):
        for ch in chains:
            ch['xp'] = _dot(ch['xp'], ch['xp']).astype(BF16)
        for ch in chains:
            ch['t'] = ch['t'] + _dot(ch['t'].astype(BF16), ch['xp'])
    for ch in chains:
        ch['s0b'] = ch['s0'].astype(BF16)
        ch['vb'] = ch['v'].astype(BF16)
        ch['z'] = _dot_nt(ch['kd'].astype(BF16), ch['s0b']) + _dot(ch['b'], ch['vb'])
    for ch in chains:
        u = _dot(ch['t'].astype(BF16), ch['z'].astype(BF16))
        ch['uv'] = jnp.concatenate([u.astype(BF16), ch['vb']], axis=0)
    for ch in chains:
        kb = jnp.concatenate([-ch['be'], ch['ke']], axis=0).astype(BF16)
        s_scr[ch['d'], ch['h']] = ch['s0'] * ch['gt'] + _dot_tn(ch['uv'], kb)
    for ch in chains:
        ch['y'] = _dot_nt(ch['rd'].astype(BF16), ch['s0b']) + _dot(ch['qp'], ch['uv'])
    yf_ref[...] = jnp.concatenate([ch['y'] for ch in chains[:H]], axis=1)
    yb_ref[...] = jnp.concatenate([ch['y'] for ch in chains[H:]], axis=1)

    @pl.when(c == nc - 1)
    def _():
        sout_ref[0] = s_scr[...]


def _dot_exact_lhs(a_exact_bf16, b, n):
    acc = None
    for t in _split_bf16(b, n):
        p = _dot(a_exact_bf16, t)
        acc = p if acc is None else acc + p
    return acc


def _causal_masks():
    t = np.arange(CHUNK)
    fwd = (t[None, :] <= t[:, None]).astype(np.float32)
    return jnp.asarray(np.stack([fwd, fwd.T]))


def _rwkv_scan(prep, row0, n_batch, T, s0):
    r, kh, v, _, _, lw, kt, beta = prep
    C = CHUNK
    nc = T // C
    c0 = row0 // C
    W = RWKV_WIDTH
    fwd = lambda b, c: b * nc + c
    bwd = lambda b, c: b * nc + nc - 1 - c
    shared_f = pl.BlockSpec((C, W), lambda b, c: (c0 + fwd(b, c), 0))
    shared_b = pl.BlockSpec((C, W), lambda b, c: (c0 + bwd(b, c), 0))
    dir_f = pl.BlockSpec((1, C, W), lambda b, c: (0, c0 + fwd(b, c), 0))
    dir_b = pl.BlockSpec((1, C, W), lambda b, c: (1, c0 + bwd(b, c), 0))
    st = pl.BlockSpec((1, 2, RWKV_HEADS, RWKV_HEAD_DIM, RWKV_HEAD_DIM), lambda b, c: (b, 0, 0, 0, 0))
    y_shape = jax.ShapeDtypeStruct((n_batch * T, W), F32)
    return pl.pallas_call(
        _rwkv_scan_kernel,
        grid=(n_batch, nc),
        in_specs=[shared_f] * 3 + [shared_b] * 3 + [dir_f] * 3 + [dir_b] * 3
                 + [pl.BlockSpec((2, C, C), lambda b, c: (0, 0, 0)), st],
        out_specs=[pl.BlockSpec((C, W), lambda b, c: (fwd(b, c), 0)),
                   pl.BlockSpec((C, W), lambda b, c: (bwd(b, c), 0)), st],
        out_shape=[y_shape, y_shape, jax.ShapeDtypeStruct(s0.shape, F32)],
        scratch_shapes=[pltpu.VMEM((2, RWKV_HEADS, RWKV_HEAD_DIM, RWKV_HEAD_DIM), F32)],
        compiler_params=_params(("parallel", "arbitrary"), 24 * 1024 * 1024),
        name="rwkv_scan",
    )(r, kh, v, r, kh, v, lw, kt, beta, lw, kt, beta, _causal_masks(), s0)


def _two_part_specs(block, tm, n_ctx_rows):
    n_ctx_tiles = n_ctx_rows // tm
    return (pl.BlockSpec(block, lambda i: (jnp.minimum(i, n_ctx_tiles - 1), 0)),
            pl.BlockSpec(block, lambda i: (jnp.maximum(i - n_ctx_tiles, 0), 0)))


def _rwkv_fin_kernel(yfc_ref, yfl_ref, ybc_ref, ybl_ref, bonus_ref, g_ref, gg_ref, gb_ref, hm_ref, o_ref,
                     *, n_ctx_tiles):
    def run(yf_ref, yb_ref):
        y = yf_ref[...] + yb_ref[...]
        mu = _dot_exact_rhs(y, hm_ref[...], 3)
        yc = y - mu
        var = _dot_exact_rhs(yc * yc, hm_ref[...], 3)
        yn = yc * lax.rsqrt(var + GN_EPS) * gg_ref[...] + gb_ref[...]
        o_ref[...] = ((yn + bonus_ref[...]) * g_ref[...]).astype(o_ref.dtype)

    is_ctx = pl.program_id(0) < n_ctx_tiles
    pl.when(is_ctx)(lambda: run(yfc_ref, ybc_ref))
    pl.when(jnp.logical_not(is_ctx))(lambda: run(yfl_ref, ybl_ref))


def _rwkv_finalize(yf_parts, yb_parts, bonus, g, gn_g, gn_b):
    n_ctx = yf_parts[0].shape[0]
    R = n_ctx + yf_parts[1].shape[0]
    tm = 512
    W = RWKV_WIDTH
    rows = pl.BlockSpec((tm, W), lambda i: (i, 0))
    vec = pl.BlockSpec((1, W), lambda i: (0, 0))
    part_c, part_l = _two_part_specs((tm, W), tm, n_ctx)
    return pl.pallas_call(
        functools.partial(_rwkv_fin_kernel, n_ctx_tiles=n_ctx // tm),
        grid=(R // tm,),
        in_specs=[part_c, part_l, part_c, part_l, rows, rows, vec, vec, pl.BlockSpec((W, W), lambda i: (0, 0))],
        out_specs=rows,
        out_shape=jax.ShapeDtypeStruct((R, W), BF16),
        compiler_params=_params(("parallel",), 24 * tm * W * 4),
        name="rwkv_finalize",
    )(yf_parts[0], yf_parts[1], yb_parts[0], yb_parts[1], bonus, g, gn_g.reshape(1, W), gn_b.reshape(1, W),
      _head_sum_matrix(1.0 / RWKV_HEAD_DIM))


def _outproj_kernel(ac_ref, al_ref, fc_ref, fl_ref, rw_ref, w_ref, x_ref, g_ref, mod_ref, o_ref, *, n_ctx_tiles):
    a0 = MLA_HEADS * MLA_V
    a1 = a0 + FNET_WIDTH

    def run(attn_ref, four_ref):
        acc = _dot(attn_ref[...], w_ref[0:a0, :])
        acc += _dot(four_ref[...], w_ref[a0:a1, :])
        acc += _dot(rw_ref[...], w_ref[a1:, :])
        o_ref[...] = x_ref[...] + mod_ref[0, 2:3, :] * _rms(acc, g_ref[...])

    is_ctx = pl.program_id(0) < n_ctx_tiles
    pl.when(is_ctx)(lambda: run(ac_ref, fc_ref))
    pl.when(jnp.logical_not(is_ctx))(lambda: run(al_ref, fl_ref))


def _out_projection(attn_parts, four_parts, rw, w_bf16, l, x, g, modg):
    R, D = x.shape
    tm = 256
    K = w_bf16.shape[1]
    n_ctx = attn_parts[0].shape[0]
    attn_c, attn_l = _two_part_specs((tm, attn_parts[0].shape[1]), tm, n_ctx)
    four_c, four_l = _two_part_specs((tm, four_parts[0].shape[1]), tm, n_ctx)
    return pl.pallas_call(
        functools.partial(_outproj_kernel, n_ctx_tiles=n_ctx // tm),
        grid=(R // tm,),
        in_specs=[attn_c, attn_l, four_c, four_l,
                  pl.BlockSpec((tm, rw.shape[1]), lambda i: (i, 0)),
                  pl.BlockSpec((None, K, D), lambda i: (l, 0, 0)),
                  pl.BlockSpec((tm, D), lambda i: (i, 0)),
                  pl.BlockSpec((1, D), lambda i: (0, 0)),
                  pl.BlockSpec((1, 6, D), lambda i: (i * (tm // ROW_GROUP), 0, 0))],
        out_specs=pl.BlockSpec((tm, D), lambda i: (i, 0)),
        out_shape=jax.ShapeDtypeStruct((R, D), F32),
        compiler_params=_params(("parallel",), 2 * K * D * 2 + 8 * tm * D * 4),
        name="out_projection",
    )(attn_parts[0], attn_parts[1], four_parts[0], four_parts[1], rw, w_bf16, x, g.reshape(1, D), modg)


FFN_SUB = 256
FFN_ACC_COLS = 512


def _ffn_kernel(x_ref, gpre_ref, mod_ref, wg_ref, wv_ref, cwg_ref, cwv_ref, cbg_ref, cbv_ref, wd_ref, gpost_ref,
                o_ref, a_scr, acc_scr, *, n_ctx_tiles):
    j = pl.program_id(1)

    @pl.when(j == 0)
    def _():
        h = _rms(x_ref[...], gpre_ref[...]) * (1.0 + mod_ref[0, 4:5, :]) + mod_ref[0, 3:4, :]
        a_scr[...] = h.astype(BF16)
        acc_scr[...] = jnp.zeros_like(acc_scr)

    tm, D = acc_scr.shape
    tn = wg_ref.shape[1]
    seq = jnp.where(pl.program_id(0) < n_ctx_tiles, SEQ, DEC_SEQ)
    pos = lax.broadcasted_iota(jnp.int32, (tm, FFN_SUB), 0) & (seq - 1)
    keep_prev = (pos != 0).astype(F32)
    keep_next = (pos != seq - 1).astype(F32)
    a = a_scr[...]

    def conv(u, cw_ref, cb_ref, cs):
        u_prev = pltpu.roll(u, 1, 0) * keep_prev
        u_next = pltpu.roll(u, tm - 1, 0) * keep_next
        return u_prev * cw_ref[0:1, cs] + u * cw_ref[1:2, cs] + u_next * cw_ref[2:3, cs] + cb_ref[:, cs]

    subs = [slice(s * FFN_SUB, (s + 1) * FFN_SUB) for s in range(tn // FFN_SUB)]
    ups = [(_dot(a, wg_ref[:, cs]), _dot(a, wv_ref[:, cs])) for cs in subs]
    for cs, (ug, uv) in zip(subs, ups):
        gate = conv(ug, cwg_ref, cbg_ref, cs)
        val = conv(uv, cwv_ref, cbv_ref, cs)
        act = (gate * _sigmoid(gate) * val).astype(BF16)
        for n in range(D // FFN_ACC_COLS):
            ns = slice(n * FFN_ACC_COLS, (n + 1) * FFN_ACC_COLS)
            acc_scr[:, ns] += _dot(act, wd_ref[cs, ns])

    @pl.when(j == pl.num_programs(1) - 1)
    def _():
        o_ref[...] = x_ref[...] + mod_ref[0, 5:6, :] * _rms(acc_scr[...], gpost_ref[...])


def _conv_ffn(x, g_pre, g_post, modg, w_up_bf16, w_down_bf16, l, conv_w, conv_b, n_ctx_rows):
    R, D = x.shape
    tm, tn = DEC_SEQ, 512
    nj = D_FF // tn
    cb = conv_b.reshape(1, 2 * D_FF)
    return pl.pallas_call(
        functools.partial(_ffn_kernel, n_ctx_tiles=n_ctx_rows // tm),
        grid=(R // tm, nj),
        in_specs=[pl.BlockSpec((tm, D), lambda i, j: (i, 0), pipeline_mode=pl.Buffered(1)),
                  pl.BlockSpec((1, D), lambda i, j: (0, 0)),
                  pl.BlockSpec((1, 6, D), lambda i, j: (i * (tm // ROW_GROUP), 0, 0)),
                  pl.BlockSpec((None, D, tn), lambda i, j: (l, 0, j)),
                  pl.BlockSpec((None, D, tn), lambda i, j: (l, 0, j + nj)),
                  pl.BlockSpec((3, tn), lambda i, j: (0, j)),
                  pl.BlockSpec((3, tn), lambda i, j: (0, j + nj)),
                  pl.BlockSpec((1, tn), lambda i, j: (0, j)),
                  pl.BlockSpec((1, tn), lambda i, j: (0, j + nj)),
                  pl.BlockSpec((None, tn, D), lambda i, j: (l, j, 0)),
                  pl.BlockSpec((1, D), lambda i, j: (0, 0))],
        out_specs=pl.BlockSpec((tm, D), lambda i, j: (i, 0), pipeline_mode=pl.Buffered(1)),
        out_shape=jax.ShapeDtypeStruct((R, D), F32),
        scratch_shapes=[pltpu.VMEM((tm, D), BF16), pltpu.VMEM((tm, D), F32)],
        compiler_params=_params(("parallel", "arbitrary"),
                                3 * tm * D * 4 + tm * D * 2 + 6 * D * tn * 2 + 16 * tm * FFN_SUB * 4),
        name="conv_ffn",
    )(x, g_pre.reshape(1, D), modg, w_up_bf16, w_up_bf16, conv_w, conv_w, cb, cb, w_down_bf16,
      g_post.reshape(1, D))


def _layout_w_in(w):
    o = np.cumsum([0, Q_LORA, KV_LORA, MLA_ROPE, FNET_WIDTH, 3 * RWKV_WIDTH, W_LORA, A_LORA, G_LORA])
    q_dn, kv_dn, k_rope, xf, rkv, w_lo, a_lo, g_lo = [w[..., o[i]:o[i + 1]] for i in range(8)]
    zeros = jnp.zeros(w.shape[:-1] + (LANE - MLA_ROPE,), w.dtype)
    return jnp.concatenate([rkv, q_dn, xf, kv_dn, k_rope, zeros, w_lo, a_lo, g_lo], axis=-1).astype(BF16)


def _layout_w_uq(w):
    L = w.shape[0]
    w = w.reshape(L, Q_LORA, MLA_HEADS, MLA_NOPE + MLA_ROPE)
    w = jnp.pad(w, ((0, 0), (0, 0), (0, 0), (0, HEAD_PAD - MLA_NOPE - MLA_ROPE)))
    return w.reshape(L, Q_LORA, MLA_HEADS * HEAD_PAD).astype(BF16)


def _layout_w_ukv(w):
    L = w.shape[0]
    w = w.reshape(L, KV_LORA, MLA_HEADS, MLA_NOPE + MLA_V)
    wk = w[..., :MLA_NOPE].reshape(L, KV_LORA, MLA_HEADS * MLA_NOPE)
    wv = w[..., MLA_NOPE:].reshape(L, KV_LORA, MLA_HEADS * MLA_V)
    return wk.astype(BF16), wv.astype(BF16)


def _rope_table(tm):
    t = jnp.arange(DEC_SEQ)
    rowp = (t // GRID_W).astype(F32)
    colp = (t % GRID_W).astype(F32)
    n = MLA_ROPE // 4
    inv = ROPE_BASE ** (-jnp.arange(n, dtype=F32) / n)
    ang = jnp.concatenate([rowp[:, None] * inv, colp[:, None] * inv], axis=-1)
    cos = jnp.concatenate([jnp.ones((tm, MLA_ROPE // 2), F32), jnp.cos(ang)], axis=0)
    sin = jnp.concatenate([jnp.zeros((tm, MLA_ROPE // 2), F32), jnp.sin(ang)], axis=0)
    z32 = jnp.zeros_like(cos)
    z64 = jnp.zeros((cos.shape[0], LANE - MLA_ROPE), F32)
    mul = jnp.concatenate([cos, cos, z64], axis=1)
    left = jnp.concatenate([-sin, z32, z64], axis=1)
    right = jnp.concatenate([z32, sin, z64], axis=1)
    return jnp.stack([mul, left, right])


def kernel(x_prompt, x_sample, cache_mla_ckv, cache_mla_krope, state_rwkv, c, c_ctx, w_mod, b_mod, g_pre_mix, g_post_mix, g_pre_ffn, g_post_ffn, w_in, g_q_norm, w_uq, g_kv_norm, w_ukv, rwkv_conv, rwkv_w0, rwkv_w2, rwkv_a0, rwkv_a2, rwkv_g2, rwkv_k_k, rwkv_k_a, rwkv_r_k, rwkv_gn_g, rwkv_gn_b, w_out, ffn_w_up, ffn_conv, ffn_conv_b, ffn_w_down):
    Bc, Tc, D = x_prompt.shape
    Bl, Tl, _ = x_sample.shape
    L = w_mod.shape[0]
    assert (Tc, Tl, D) == (SEQ, DEC_SEQ, D_MODEL) and cache_mla_ckv.shape[2] == PAST_LEN
    assert Bl + 1 <= 8 and (Bc * Tc) % DEC_SEQ == 0
    n_ctx = Bc * Tc
    n_lat = Bl * Tl
    x = jnp.concatenate([x_prompt.reshape(n_ctx, D), x_sample.reshape(n_lat, D)], axis=0)

    cond = jnp.zeros((8, D), F32).at[0].set(c_ctx).at[1:1 + Bl].set(c)
    mods = _modulation(cond, w_mod, b_mod)
    group_row = np.concatenate([np.zeros(n_ctx // ROW_GROUP, np.int32),
                                1 + np.arange(n_lat // ROW_GROUP, dtype=np.int32) // (Tl // ROW_GROUP)])

    tm_mla = 512
    rope_tab = _rope_table(tm_mla)
    ridx = _rope_index(tm_mla, n_ctx)
    ridx_cache = lambda i: 0
    zero_state = jnp.zeros((Bc, 2, RWKV_HEADS, RWKV_HEAD_DIM, RWKV_HEAD_DIM), F32)

    w_in_b = _layout_w_in(w_in)
    w_uq_b = _layout_w_uq(w_uq)
    wk_b, wv_b = _layout_w_ukv(w_ukv)
    w_out_b = w_out.astype(BF16)
    w_up_b = ffn_w_up.astype(BF16)
    w_down_b = ffn_w_down.astype(BF16)

    ckv_out, krope_out, state_out = [], [], []
    for l in range(L):
        modg = mods[l][group_row].reshape(-1, 6, D)
        p = dict(rwkv_conv=rwkv_conv[l], rwkv_w0=rwkv_w0[l], rwkv_w2=rwkv_w2[l], rwkv_a0=rwkv_a0[l],
                 rwkv_a2=rwkv_a2[l], rwkv_g2=rwkv_g2[l], rwkv_k_k=rwkv_k_k[l], rwkv_k_a=rwkv_k_a[l],
                 rwkv_r_k=rwkv_r_k[l])

        hin = _in_projection(x, g_pre_mix[l], modg, w_in_b, l)

        q = _q_projection(hin, g_q_norm[l], w_uq_b, l, rope_tab, n_ctx)
        ckv, k, v = _kv_projection(hin, COL_KVDN // KV_LORA, COL_KROPE // LANE, g_kv_norm[l], wk_b, wv_b, l,
                                   rope_tab, ridx, tm_mla, True)
        cache_src = jnp.concatenate(
            [cache_mla_ckv[:, l].reshape(Bl * PAST_LEN, KV_LORA),
             jnp.pad(cache_mla_krope[:, l].reshape(Bl * PAST_LEN, MLA_ROPE), ((0, 0), (0, LANE - MLA_ROPE)))],
            axis=1)
        k_c, v_c = _kv_projection(cache_src, 0, KV_LORA // LANE, g_kv_norm[l], wk_b, wv_b, l,
                                  rope_tab, ridx_cache, PAST_LEN, False)
        attn = (_attention(q, 0, Bc, Tc, [(k, v, 0, Tc)]),
                _attention(q, n_ctx, Bl, Tl, [(k_c, v_c, 0, PAST_LEN), (k, v, n_ctx, Tl)]))

        four = (_fourier_mix(hin, 0, Bc, Tc), _fourier_mix(hin, n_ctx, Bl, Tl))

        prep = _rwkv_prep(hin, p, n_ctx)
        yf_ctx, yb_ctx, s_ctx = _rwkv_scan(prep, 0, Bc, Tc, zero_state)
        yf_lat, yb_lat, _ = _rwkv_scan(prep, n_ctx, Bl, Tl, state_rwkv[:, l])
        rw = _rwkv_finalize((yf_ctx, yf_lat), (yb_ctx, yb_lat), prep[4], prep[3], rwkv_gn_g[l], rwkv_gn_b[l])

        x = _out_projection(attn, four, rw, w_out_b, l, x, g_post_mix[l], modg)
        x = _conv_ffn(x, g_pre_ffn[l], g_post_ffn[l], modg, w_up_b, w_down_b, l, ffn_conv[l], ffn_conv_b[l], n_ctx)

        ckv_out.append(ckv[:n_ctx].reshape(Bc, Tc, KV_LORA))
        krope_out.append(hin[:n_ctx, COL_KROPE:COL_KROPE + MLA_ROPE].reshape(Bc, Tc, MLA_ROPE))
        state_out.append(s_ctx)

    return (x[:n_ctx].reshape(Bc, Tc, D), x[n_ctx:].reshape(Bl, Tl, D),
            jnp.stack(ckv_out, axis=1), jnp.stack(krope_out, axis=1), jnp.stack(state_out, axis=1))
```

```python
import functools
import math

import jax
import jax.numpy as jnp
import numpy as np
from jax import lax
from jax.experimental import pallas as pl
from jax.experimental.pallas import tpu as pltpu

F32 = jnp.float32
BF16 = jnp.bfloat16

D_MODEL = 2048
SEQ = 256
DEC_SEQ = 1024
PAST_LEN = 256
GRID_W = 64
MLA_HEADS = 8
MLA_NOPE = 128
MLA_ROPE = 64
MLA_V = 128
Q_LORA = 512
KV_LORA = 256
FNET_GROUPS = 4
FNET_GROUP_DIM = 128
FNET_WIDTH = FNET_GROUPS * FNET_GROUP_DIM
RWKV_HEADS = 8
RWKV_HEAD_DIM = 64
RWKV_WIDTH = RWKV_HEADS * RWKV_HEAD_DIM
W_LORA = 64
A_LORA = 64
G_LORA = 128
D_FF = 5632
ROPE_BASE = 10000.0
EPS = 1e-6
GN_EPS = 64e-5
DECAY_SCALE = math.exp(-0.5)

LANE = 128
ROW_GROUP = 256
HEAD_PAD = 256
CHUNK = 64
VMEM_CAP = 60 * 1024 * 1024

COL_RKV = 0
COL_QDN = 3 * RWKV_WIDTH
COL_XF = COL_QDN + Q_LORA
COL_KVDN = COL_XF + FNET_WIDTH
COL_KROPE = COL_KVDN + KV_LORA
COL_WA = COL_KROPE + LANE
COL_GLO = COL_WA + LANE
MXU_COLS = 256
IN_PAD = -(-(COL_GLO + G_LORA) // MXU_COLS) * MXU_COLS


SMALL_OPERANDS = 8 * 1024 * 1024


def _params(sem, nbytes):
    return pltpu.CompilerParams(dimension_semantics=sem,
                                vmem_limit_bytes=int(min(VMEM_CAP, nbytes + SMALL_OPERANDS)))


def _dot(a, b):
    return jnp.dot(a, b, preferred_element_type=F32)


def _dot_nt(a, b):
    return lax.dot_general(a, b, (((1,), (1,)), ((), ())), preferred_element_type=F32)


def _dot_tn(a, b):
    return lax.dot_general(a, b, (((0,), (0,)), ((), ())), preferred_element_type=F32)


def _split_bf16(x, n):
    terms = []
    rem = x
    for i in range(n):
        t = rem.astype(BF16)
        terms.append(t)
        if i + 1 < n:
            rem = rem - t.astype(F32)
    return terms


def _dot_exact_rhs(a, b_exact_bf16, n):
    acc = None
    for t in _split_bf16(a, n):
        p = _dot(t, b_exact_bf16)
        acc = p if acc is None else acc + p
    return acc


def _rms(x, g):
    return x * lax.rsqrt(jnp.mean(x * x, axis=-1, keepdims=True) + EPS) * g


def _sigmoid(x):
    return 1.0 / (1.0 + jnp.exp(-x))


def _mod_kernel(cond_ref, w_ref, b_ref, o_ref):
    c = cond_ref[...]
    s = (c * _sigmoid(c)).astype(BF16)
    o_ref[0] = _dot(s, w_ref[0].astype(BF16)) + b_ref[0]


def _modulation(cond, w_mod, b_mod):
    L, D, N = w_mod.shape
    tn = 1024
    return pl.pallas_call(
        _mod_kernel,
        grid=(L, N // tn),
        in_specs=[pl.BlockSpec((8, D), lambda l, j: (0, 0)),
                  pl.BlockSpec((1, D, tn), lambda l, j: (l, 0, j)),
                  pl.BlockSpec((1, 1, tn), lambda l, j: (l, 0, j))],
        out_specs=pl.BlockSpec((1, 8, tn), lambda l, j: (l, 0, j)),
        out_shape=jax.ShapeDtypeStruct((L, 8, N), F32),
        compiler_params=_params(("parallel", "parallel"), 3 * D * tn * 4),
        name="modulation",
    )(cond, w_mod, b_mod.reshape(L, 1, N))


def _inproj_kernel(x_ref, g_ref, mod_ref, w_ref, o_ref):
    h = _rms(x_ref[...], g_ref[...]) * (1.0 + mod_ref[0, 1:2, :]) + mod_ref[0, 0:1, :]
    o_ref[...] = _dot(h.astype(BF16), w_ref[...])


def _in_projection(x, g, modg, w_bf16, l):
    R, D = x.shape
    N = w_bf16.shape[2]
    tm = 512
    return pl.pallas_call(
        _inproj_kernel,
        grid=(R // tm,),
        in_specs=[pl.BlockSpec((tm, D), lambda i: (i, 0)),
                  pl.BlockSpec((1, D), lambda i: (0, 0)),
                  pl.BlockSpec((1, 6, D), lambda i: (i * (tm // ROW_GROUP), 0, 0)),
                  pl.BlockSpec((None, D, N), lambda i: (l, 0, 0), pipeline_mode=pl.Buffered(1))],
        out_specs=pl.BlockSpec((tm, N), lambda i: (i, 0)),
        out_shape=jax.ShapeDtypeStruct((R, N), F32),
        compiler_params=_params(("parallel",), 2 * tm * D * 4 + tm * D * 2 + D * N * 2 + 3 * tm * N * 4),
        name="in_projection",
    )(x, g.reshape(1, D), modg, w_bf16)


def _rope_block(blk, rope_ref):
    return (blk * rope_ref[0]
            + pltpu.roll(blk, LANE - MLA_ROPE // 2, 1) * rope_ref[1]
            + pltpu.roll(blk, MLA_ROPE // 2, 1) * rope_ref[2])


def _q_kernel(qdn_ref, g_ref, w_ref, rope_ref, o_ref):
    scale = (MLA_NOPE + MLA_ROPE) ** -0.5
    qn = _rms(qdn_ref[...], g_ref[...]).astype(BF16)
    q = _dot(qn, w_ref[...]) * scale
    parts = []
    for h in range(MLA_HEADS):
        base = h * HEAD_PAD
        parts.append(q[:, base:base + MLA_NOPE])
        parts.append(_rope_block(q[:, base + MLA_NOPE:base + HEAD_PAD], rope_ref))
    o_ref[...] = jnp.concatenate(parts, axis=1).astype(o_ref.dtype)


def _rope_index(tm, n_ctx_rows):
    n_ctx_tiles = n_ctx_rows // tm
    per_seq = DEC_SEQ // tm

    def index(i):
        return jnp.where(i < n_ctx_tiles, 0, 1 + (i - n_ctx_tiles) % per_seq)
    return index


def _q_projection(hin, g, w_bf16, l, rope_tab, n_ctx_rows):
    R = hin.shape[0]
    tm = 512
    N = MLA_HEADS * HEAD_PAD
    ridx = _rope_index(tm, n_ctx_rows)
    return pl.pallas_call(
        _q_kernel,
        grid=(R // tm,),
        in_specs=[pl.BlockSpec((tm, Q_LORA), lambda i: (i, COL_QDN // Q_LORA)),
                  pl.BlockSpec((1, Q_LORA), lambda i: (0, 0)),
                  pl.BlockSpec((None, Q_LORA, N), lambda i: (l, 0, 0)),
                  pl.BlockSpec((3, tm, LANE), lambda i: (0, ridx(i), 0))],
        out_specs=pl.BlockSpec((tm, N), lambda i: (i, 0)),
        out_shape=jax.ShapeDtypeStruct((R, N), BF16),
        compiler_params=_params(("parallel",), 2 * Q_LORA * N * 2 + 6 * tm * N * 4),
        name="q_projection",
    )(hin, g.reshape(1, Q_LORA), w_bf16, rope_tab)


def _kv_kernel(kvdn_ref, kr_ref, g_ref, wk_ref, wv_ref, rope_ref, *out_refs, normalize):
    k_ref, v_ref = out_refs[-2:]
    c = kvdn_ref[...]
    if normalize:
        c = _rms(c, g_ref[...])
        out_refs[0][...] = c
    cb = c.astype(BF16)
    kn = _dot(cb, wk_ref[...])
    v_ref[...] = _dot(cb, wv_ref[...]).astype(v_ref.dtype)
    kr = _rope_block(kr_ref[...], rope_ref)
    parts = []
    for h in range(MLA_HEADS):
        parts.append(kn[:, h * MLA_NOPE:(h + 1) * MLA_NOPE])
        parts.append(kr)
    k_ref[...] = jnp.concatenate(parts, axis=1).astype(k_ref.dtype)


def _kv_projection(src, kv_col, kr_col, g, wk_bf16, wv_bf16, l, rope_tab, ridx, tm, normalize):
    R = src.shape[0]
    NK = MLA_HEADS * HEAD_PAD
    NV = MLA_HEADS * MLA_V
    return pl.pallas_call(
        functools.partial(_kv_kernel, normalize=normalize),
        grid=(R // tm,),
        in_specs=[pl.BlockSpec((tm, KV_LORA), lambda i: (i, kv_col)),
                  pl.BlockSpec((tm, LANE), lambda i: (i, kr_col)),
                  pl.BlockSpec((1, KV_LORA), lambda i: (0, 0)),
                  pl.BlockSpec((None, KV_LORA, MLA_HEADS * MLA_NOPE), lambda i: (l, 0, 0)),
                  pl.BlockSpec((None, KV_LORA, NV), lambda i: (l, 0, 0)),
                  pl.BlockSpec((3, tm, LANE), lambda i: (0, ridx(i), 0))],
        out_specs=([pl.BlockSpec((tm, KV_LORA), lambda i: (i, 0))] if normalize else [])
                  + [pl.BlockSpec((tm, NK), lambda i: (i, 0)), pl.BlockSpec((tm, NV), lambda i: (i, 0))],
        out_shape=([jax.ShapeDtypeStruct((R, KV_LORA), F32)] if normalize else [])
                  + [jax.ShapeDtypeStruct((R, NK), BF16), jax.ShapeDtypeStruct((R, NV), BF16)],
        compiler_params=_params(("parallel",), 8 * tm * NK * 4),
        name="kv_projection" if normalize else "kv_projection_cache",
    )(src, src, g.reshape(1, KV_LORA), wk_bf16, wv_bf16, rope_tab)


def _attn_kernel(q_ref, *refs):
    o_ref = refs[-1]
    kv = [(refs[i], refs[i + 1]) for i in range(0, len(refs) - 1, 2)]
    for h in range(MLA_HEADS):
        qk = slice(h * HEAD_PAD, (h + 1) * HEAD_PAD)
        vo = slice(h * MLA_V, (h + 1) * MLA_V)
        q = q_ref[:, qk]
        scores = [_dot_nt(q, k_ref[:, qk]) for k_ref, _ in kv]
        m = functools.reduce(jnp.maximum, [jnp.max(s, axis=-1, keepdims=True) for s in scores])
        probs = [jnp.exp(s - m) for s in scores]
        l = functools.reduce(jnp.add, [jnp.sum(p, axis=-1, keepdims=True) for p in probs])
        o = functools.reduce(jnp.add, [_dot(p.astype(BF16), v_ref[:, vo]) for p, (_, v_ref) in zip(probs, kv)])
        o_ref[:, vo] = (o / l).astype(o_ref.dtype)


def _attention(q, q_row0, n_batch, tq_total, sources):
    tq = min(512, tq_total)
    nq = tq_total // tq
    q0 = q_row0 // tq
    NQ = MLA_HEADS * HEAD_PAD
    NV = MLA_HEADS * MLA_V
    in_specs = [pl.BlockSpec((tq, NQ), lambda b, i: (q0 + b * nq + i, 0))]
    args = [q]
    tk_total = 0
    for k, v, row0, tk in sources:
        in_specs.append(pl.BlockSpec((tk, NQ), lambda b, i, o=row0 // tk: (o + b, 0)))
        in_specs.append(pl.BlockSpec((tk, NV), lambda b, i, o=row0 // tk: (o + b, 0)))
        args += [k, v]
        tk_total += tk
    return pl.pallas_call(
        _attn_kernel,
        grid=(n_batch, nq),
        in_specs=in_specs,
        out_specs=pl.BlockSpec((tq, NV), lambda b, i: (b * nq + i, 0)),
        out_shape=jax.ShapeDtypeStruct((n_batch * tq_total, NV), BF16),
        compiler_params=_params(("parallel", "arbitrary"),
                                2 * (tq + tk_total) * (NQ + NV) * 2 + 6 * tq * tk_total * 4),
        name="attention",
    )(*args)


def _fnet_kernel(z_ref, cs_ref, ct_ref, o_ref):
    yc, ys = [], []
    for g in range(FNET_GROUPS):
        zg = z_ref[:, g * FNET_GROUP_DIM:(g + 1) * FNET_GROUP_DIM].astype(BF16)
        y = _dot(zg, cs_ref[...])
        yc.append(y[:, :FNET_GROUP_DIM])
        ys.append(y[:, FNET_GROUP_DIM:])
    stacked = jnp.concatenate([jnp.concatenate(yc, axis=1), jnp.concatenate(ys, axis=1)], axis=0)
    o_ref[...] = _dot(ct_ref[...], stacked.astype(BF16)).astype(o_ref.dtype)


def _dft_tables(T):
    def cs(n):
        i = jnp.arange(n, dtype=jnp.int32)
        ang = ((i[:, None] * i[None, :]) % n).astype(F32) * (2.0 * math.pi / n)
        return jnp.cos(ang), jnp.sin(ang)
    cc, sc = cs(FNET_GROUP_DIM)
    ct, st = cs(T)
    norm = 1.0 / math.sqrt(T * FNET_GROUP_DIM)
    return (jnp.concatenate([cc, sc], axis=1).astype(BF16),
            (jnp.concatenate([ct, -st], axis=1) * norm).astype(BF16))


def _fourier_mix(hin, row0, n_batch, T):
    cs_tab, ct_tab = _dft_tables(T)
    b0 = row0 // T
    return pl.pallas_call(
        _fnet_kernel,
        grid=(n_batch,),
        in_specs=[pl.BlockSpec((T, FNET_WIDTH), lambda b: (b0 + b, COL_XF // FNET_WIDTH)),
                  pl.BlockSpec((FNET_GROUP_DIM, 2 * FNET_GROUP_DIM), lambda b: (0, 0)),
                  pl.BlockSpec((T, 2 * T), lambda b: (0, 0))],
        out_specs=pl.BlockSpec((T, FNET_WIDTH), lambda b: (b, 0)),
        out_shape=jax.ShapeDtypeStruct((n_batch * T, FNET_WIDTH), BF16),
        compiler_params=_params(("parallel",), 2 * T * 2 * T * 2 + 10 * T * FNET_WIDTH * 4),
        name="fourier_mix",
    )(hin, cs_tab, ct_tab)


def _head_sum_matrix(scale):
    i = np.arange(RWKV_WIDTH) // RWKV_HEAD_DIM
    return jnp.asarray((i[:, None] == i[None, :]).astype(np.float32) * scale, BF16)


def _rwkv_prep_kernel(x_ref, xp_ref, xn_ref, wa_ref, gl_ref, cw_ref, w0_ref, w2_ref, a0_ref, a2_ref, g2_ref,
                      kk_ref, ka_ref, rk_ref, hs_ref,
                      r_ref, kh_ref, v_ref, g_ref, bonus_ref, lw_ref, kt_ref, beta_ref,
                      *, n_ctx_tiles, tiles_per_seq):
    i = pl.program_id(0)
    j = i - n_ctx_tiles
    first = jnp.logical_or(i < n_ctx_tiles, j % tiles_per_seq == 0)
    last = jnp.logical_or(i < n_ctx_tiles, j % tiles_per_seq == tiles_per_seq - 1)
    x = x_ref[...]
    tm = x.shape[0]
    row = lax.broadcasted_iota(jnp.int32, x.shape, 0)
    prev_row = jnp.where(first, 0.0, xp_ref[7:8, :])
    next_row = jnp.where(last, 0.0, xn_ref[0:1, :])
    x_prev = jnp.where(row == 0, prev_row, pltpu.roll(x, 1, 0))
    x_next = jnp.where(row == tm - 1, next_row, pltpu.roll(x, tm - 1, 0))
    xc = x_prev * cw_ref[0:1, :] + x * cw_ref[1:2, :] + x_next * cw_ref[2:3, :]
    R_ = RWKV_WIDTH
    r, k, v = xc[:, :R_], xc[:, R_:2 * R_], xc[:, 2 * R_:]
    wl = jnp.tanh(wa_ref[:, :W_LORA]).astype(BF16)
    al = wa_ref[:, W_LORA:W_LORA + A_LORA].astype(BF16)
    g_ref[...] = _dot(_sigmoid(gl_ref[...]).astype(BF16), g2_ref[...].astype(BF16))
    kappa = k * kk_ref[...]
    ss = _dot_exact_rhs(kappa * kappa, hs_ref[...], 3)
    kh = kappa * lax.rsqrt(ss + EPS)
    kt_sum = None
    for d in range(2):
        lw_ref[d] = -DECAY_SCALE * _sigmoid(w0_ref[d:d + 1, :] + _dot(wl, w2_ref[d].astype(BF16)))
        a = _sigmoid(a0_ref[d:d + 1, :] + _dot(al, a2_ref[d].astype(BF16)))
        kt = k * (1.0 + (a - 1.0) * ka_ref[...])
        kt_ref[d] = kt
        beta_ref[d] = a * kh
        kt_sum = kt if kt_sum is None else kt_sum + kt
    r_ref[...] = r
    kh_ref[...] = kh
    v_ref[...] = v
    bonus_ref[...] = _dot_exact_rhs(r * kt_sum * rk_ref[...], hs_ref[...], 3) * v


def _rwkv_prep(hin, p, n_ctx_rows):
    R = hin.shape[0]
    tm = ROW_GROUP
    W3 = 3 * RWKV_WIDTH
    nh = tm // 8
    last_blk = R // 8 - 1
    row = lambda a: a.reshape(1, RWKV_WIDTH)
    full = lambda *s: pl.BlockSpec(s, lambda i: (0,) * len(s))
    out_rw = jax.ShapeDtypeStruct((R, RWKV_WIDTH), F32)
    out_rw2 = jax.ShapeDtypeStruct((2, R, RWKV_WIDTH), F32)
    spec_rw = pl.BlockSpec((tm, RWKV_WIDTH), lambda i: (i, 0))
    spec_rw2 = pl.BlockSpec((2, tm, RWKV_WIDTH), lambda i: (0, i, 0))
    return pl.pallas_call(
        functools.partial(_rwkv_prep_kernel, n_ctx_tiles=n_ctx_rows // tm, tiles_per_seq=DEC_SEQ // tm),
        grid=(R // tm,),
        in_specs=[pl.BlockSpec((tm, W3), lambda i: (i, COL_RKV // W3)),
                  pl.BlockSpec((8, W3), lambda i: (jnp.maximum(i * nh - 1, 0), 0)),
                  pl.BlockSpec((8, W3), lambda i: (jnp.minimum((i + 1) * nh, last_blk), 0)),
                  pl.BlockSpec((tm, LANE), lambda i: (i, COL_WA // LANE)),
                  pl.BlockSpec((tm, G_LORA), lambda i: (i, COL_GLO // G_LORA)),
                  full(3, W3), full(2, RWKV_WIDTH), full(2, W_LORA, RWKV_WIDTH),
                  full(2, RWKV_WIDTH), full(2, A_LORA, RWKV_WIDTH), full(G_LORA, RWKV_WIDTH),
                  full(1, RWKV_WIDTH), full(1, RWKV_WIDTH), full(1, RWKV_WIDTH),
                  full(RWKV_WIDTH, RWKV_WIDTH)],
        out_specs=[spec_rw] * 5 + [spec_rw2] * 3,
        out_shape=[out_rw] * 5 + [out_rw2] * 3,
        compiler_params=_params(("parallel",), 40 * tm * W3 * 4),
        name="rwkv_prep",
    )(hin, hin, hin, hin, hin, p['rwkv_conv'], p['rwkv_w0'], p['rwkv_w2'], p['rwkv_a0'], p['rwkv_a2'],
      p['rwkv_g2'], row(p['rwkv_k_k']), row(p['rwkv_k_a']), row(p['rwkv_r_k']), _head_sum_matrix(1.0))


def _rwkv_scan_kernel(*refs, n_seq):
    seq_refs = [refs[13 * p:13 * (p + 1)] for p in range(n_seq)]
    mi_ref = refs[13 * n_seq]
    out_refs = refs[13 * n_seq + 1:-1]
    s_scr = refs[-1]
    c = pl.program_id(1)
    nc = pl.num_programs(1)
    N = RWKV_HEAD_DIM
    C = CHUNK
    H = RWKV_HEADS

    @pl.when(c == 0)
    def _():
        for p in range(n_seq):
            s_scr[p] = seq_refs[p][12][0]

    eye = (lax.broadcasted_iota(jnp.int32, (C, C), 0) == lax.broadcasted_iota(jnp.int32, (C, C), 1)).astype(F32)
    dirs = [(p, d, tuple(sr[3 * d:3 * d + 3]) + tuple(sr[6 + 3 * d:9 + 3 * d]))
            for p, sr in enumerate(seq_refs) for d in range(2)]
    chains = []
    for p, d, (r_ref, kh_ref, v_ref, lw_ref, kt_ref, beta_ref) in dirs:
        mi = mi_ref[d]
        lw = lw_ref[0]
        cl = _dot_exact_lhs(mi.astype(BF16), lw, 3)
        cl_tot = jnp.sum(lw, axis=0, keepdims=True)
        g_inv = jnp.exp(-cl)
        g_end = jnp.exp(cl_tot - cl)
        g_tot = jnp.exp(cl_tot)
        kh, beta, kt = kh_ref[...], beta_ref[0], kt_ref[0]
        k_dec = kh * jnp.exp(cl - lw)
        r_dec = r_ref[...] * jnp.exp(cl)
        b_inv, k_inv = beta * g_inv, kt * g_inv
        k_end, b_end = kt * g_end, beta * g_end
        v_all = v_ref[...]
        for h in range(H):
            sl = slice(h * N, (h + 1) * N)
            chains.append(dict(p=p, d=d, h=h, mi=mi, ms=mi - eye, kd=k_dec[:, sl], rd=r_dec[:, sl],
                               bi=b_inv[:, sl], ki=k_inv[:, sl], ke=k_end[:, sl], be=b_end[:, sl],
                               v=v_all[:, sl], gt=g_tot[:, sl], s0=s_scr[p, d, h]))

    for ch in chains:
        lhs = jnp.concatenate([ch['kd'], ch['rd']], axis=0).astype(BF16)
        rhs = jnp.concatenate([ch['bi'], ch['ki']], axis=0).astype(BF16)
        nt = _dot_nt(lhs, rhs)
        ch['a'] = nt[:C, :C] * ch['ms']
        ch['b'] = (nt[:C, C:] * ch['ms']).astype(BF16)
        ch['qp'] = (nt[C:, :] * jnp.concatenate([-ch['mi'], ch['mi']], axis=1)).astype(BF16)
        ch['xp'] = (-ch['a']).astype(BF16)
        ch['t'] = eye - ch['a']
    for _ in range(int(math.log2(C)) - 1):
        for ch in chains:
            ch['xp'] = _dot(ch['xp'], ch['xp']).astype(BF16)
        for ch in chains:
            ch['t'] = ch['t'] + _dot(ch['t'].astype(BF16), ch['xp'])
    for ch in chains:
        ch['s0b'] = ch['s0'].astype(BF16)
        ch['vb'] = ch['v'].astype(BF16)
        ch['z'] = _dot_nt(ch['kd'].astype(BF16), ch['s0b']) + _dot(ch['b'], ch['vb'])
    for ch in chains:
        u = _dot(ch['t'].astype(BF16), ch['z'].astype(BF16))
        ch['uv'] = jnp.concatenate([u.astype(BF16), ch['vb']], axis=0)
    for ch in chains:
        kb = jnp.concatenate([-ch['be'], ch['ke']], axis=0).astype(BF16)
        s_scr[ch['p'], ch['d'], ch['h']] = ch['s0'] * ch['gt'] + _dot_tn(ch['uv'], kb)
    for ch in chains:
        ch['y'] = _dot_nt(ch['rd'].astype(BF16), ch['s0b']) + _dot(ch['qp'], ch['uv'])
    for p in range(n_seq):
        for d in range(2):
            ys = [ch['y'] for ch in chains[(2 * p + d) * H:(2 * p + d + 1) * H]]
            out_refs[3 * p + d][...] = jnp.concatenate(ys, axis=1)

    @pl.when(c == nc - 1)
    def _():
        for p in range(n_seq):
            out_refs[3 * p + 2][0] = s_scr[p]


def _dot_exact_lhs(a_exact_bf16, b, n):
    acc = None
    for t in _split_bf16(b, n):
        p = _dot(a_exact_bf16, t)
        acc = p if acc is None else acc + p
    return acc


def _causal_masks():
    t = np.arange(CHUNK)
    fwd = (t[None, :] <= t[:, None]).astype(np.float32)
    return jnp.asarray(np.stack([fwd, fwd.T]))


def _rwkv_scan(prep, row0, n_batch, T, s0):
    r, kh, v, _, _, lw, kt, beta = prep
    C = CHUNK
    nc = T // C
    c0 = row0 // C
    W = RWKV_WIDTH
    n_seq = 2 if n_batch % 2 == 0 else 1
    nb = n_batch // n_seq
    st_shape = (1, 2, RWKV_HEADS, RWKV_HEAD_DIM, RWKV_HEAD_DIM)
    in_specs, args, out_specs, out_shape = [], [], [], []
    for p in range(n_seq):
        fwd = lambda b, c, p=p: (p * nb + b) * nc + c
        bwd = lambda b, c, p=p: (p * nb + b) * nc + nc - 1 - c
        shared_f = pl.BlockSpec((C, W), lambda b, c, f=fwd: (c0 + f(b, c), 0))
        shared_b = pl.BlockSpec((C, W), lambda b, c, f=bwd: (c0 + f(b, c), 0))
        dir_f = pl.BlockSpec((1, C, W), lambda b, c, f=fwd: (0, c0 + f(b, c), 0))
        dir_b = pl.BlockSpec((1, C, W), lambda b, c, f=bwd: (1, c0 + f(b, c), 0))
        in_specs += [shared_f] * 3 + [shared_b] * 3 + [dir_f] * 3 + [dir_b] * 3
        in_specs.append(pl.BlockSpec(st_shape, lambda b, c, p=p: (p * nb + b, 0, 0, 0, 0)))
        args += [r, kh, v, r, kh, v, lw, kt, beta, lw, kt, beta, s0]
        out_specs += [pl.BlockSpec((C, W), lambda b, c: (b * nc + c, 0)),
                      pl.BlockSpec((C, W), lambda b, c: (b * nc + nc - 1 - c, 0)),
                      pl.BlockSpec(st_shape, lambda b, c: (b, 0, 0, 0, 0))]
        out_shape += [jax.ShapeDtypeStruct((nb * T, W), F32)] * 2
        out_shape.append(jax.ShapeDtypeStruct((nb,) + st_shape[1:], F32))
    in_specs.append(pl.BlockSpec((2, C, C), lambda b, c: (0, 0, 0)))
    outs = pl.pallas_call(
        functools.partial(_rwkv_scan_kernel, n_seq=n_seq),
        grid=(nb, nc),
        in_specs=in_specs,
        out_specs=out_specs,
        out_shape=out_shape,
        scratch_shapes=[pltpu.VMEM((n_seq, 2, RWKV_HEADS, RWKV_HEAD_DIM, RWKV_HEAD_DIM), F32)],
        compiler_params=_params(("parallel", "arbitrary"), 32 * 1024 * 1024),
        name="rwkv_scan",
    )(*args, _causal_masks())
    return outs[0::3], outs[1::3], jnp.concatenate(outs[2::3], axis=0)


def _part_tiles(parts, tm):
    return [int(t) for t in np.cumsum([0] + [p.shape[0] // tm for p in parts])]


def _part_specs(parts, tm):
    starts = _part_tiles(parts, tm)
    return [pl.BlockSpec((tm, p.shape[1]), lambda i, t0=t0, n=t1 - t0: (jnp.clip(i - t0, 0, n - 1), 0))
            for p, t0, t1 in zip(parts, starts[:-1], starts[1:])]


def _for_own_part(starts, run):
    i = pl.program_id(0)
    for k, (t0, t1) in enumerate(zip(starts[:-1], starts[1:])):
        pl.when(jnp.logical_and(i >= t0, i < t1))(functools.partial(run, k))


def _rwkv_fin_kernel(*refs, starts):
    n = len(starts) - 1
    yf_refs, yb_refs = refs[:n], refs[n:2 * n]
    bonus_ref, g_ref, gg_ref, gb_ref, hm_ref, o_ref = refs[2 * n:]

    def run(k):
        y = yf_refs[k][...] + yb_refs[k][...]
        mu = _dot_exact_rhs(y, hm_ref[...], 3)
        yc = y - mu
        var = _dot_exact_rhs(yc * yc, hm_ref[...], 3)
        yn = yc * lax.rsqrt(var + GN_EPS) * gg_ref[...] + gb_ref[...]
        o_ref[...] = ((yn + bonus_ref[...]) * g_ref[...]).astype(o_ref.dtype)

    _for_own_part(starts, run)


def _rwkv_finalize(yf_parts, yb_parts, bonus, g, gn_g, gn_b):
    tm = 512
    W = RWKV_WIDTH
    starts = _part_tiles(yf_parts, tm)
    rows = pl.BlockSpec((tm, W), lambda i: (i, 0))
    vec = pl.BlockSpec((1, W), lambda i: (0, 0))
    return pl.pallas_call(
        functools.partial(_rwkv_fin_kernel, starts=starts),
        grid=(starts[-1],),
        in_specs=_part_specs(yf_parts, tm) + _part_specs(yb_parts, tm)
                 + [rows, rows, vec, vec, pl.BlockSpec((W, W), lambda i: (0, 0))],
        out_specs=rows,
        out_shape=jax.ShapeDtypeStruct((starts[-1] * tm, W), BF16),
        compiler_params=_params(("parallel",), 24 * tm * W * 4),
        name="rwkv_finalize",
    )(*yf_parts, *yb_parts, bonus, g, gn_g.reshape(1, W), gn_b.reshape(1, W),
      _head_sum_matrix(1.0 / RWKV_HEAD_DIM))


def _outproj_kernel(ac_ref, al_ref, fc_ref, fl_ref, rw_ref, w_ref, x_ref, g_ref, mod_ref, o_ref, *, starts):
    a0 = MLA_HEADS * MLA_V
    a1 = a0 + FNET_WIDTH

    def run(k):
        attn_ref, four_ref = ((ac_ref, fc_ref), (al_ref, fl_ref))[k]
        acc = _dot(attn_ref[...], w_ref[0:a0, :])
        acc += _dot(four_ref[...], w_ref[a0:a1, :])
        acc += _dot(rw_ref[...], w_ref[a1:, :])
        o_ref[...] = x_ref[...] + mod_ref[0, 2:3, :] * _rms(acc, g_ref[...])

    _for_own_part(starts, run)


def _out_projection(attn_parts, four_parts, rw, w_bf16, l, x, g, modg):
    R, D = x.shape
    tm = 256
    K = w_bf16.shape[1]
    return pl.pallas_call(
        functools.partial(_outproj_kernel, starts=_part_tiles(attn_parts, tm)),
        grid=(R // tm,),
        in_specs=_part_specs(attn_parts, tm) + _part_specs(four_parts, tm)
                 + [pl.BlockSpec((tm, rw.shape[1]), lambda i: (i, 0)),
                  pl.BlockSpec((None, K, D), lambda i: (l, 0, 0)),
                  pl.BlockSpec((tm, D), lambda i: (i, 0)),
                  pl.BlockSpec((1, D), lambda i: (0, 0)),
                  pl.BlockSpec((1, 6, D), lambda i: (i * (tm // ROW_GROUP), 0, 0))],
        out_specs=pl.BlockSpec((tm, D), lambda i: (i, 0)),
        out_shape=jax.ShapeDtypeStruct((R, D), F32),
        compiler_params=_params(("parallel",), 2 * K * D * 2 + 8 * tm * D * 4),
        name="out_projection",
    )(attn_parts[0], attn_parts[1], four_parts[0], four_parts[1], rw, w_bf16, x, g.reshape(1, D), modg)


FFN_SUB = 256
FFN_ACC_COLS = 512


def _ffn_kernel(x_ref, gpre_ref, mod_ref, wg_ref, wv_ref, cwg_ref, cwv_ref, cbg_ref, cbv_ref, wd_ref, gpost_ref,
                o_ref, a_scr, acc_scr, *, n_ctx_tiles):
    j = pl.program_id(1)

    @pl.when(j == 0)
    def _():
        h = _rms(x_ref[...], gpre_ref[...]) * (1.0 + mod_ref[0, 4:5, :]) + mod_ref[0, 3:4, :]
        a_scr[...] = h.astype(BF16)
        acc_scr[...] = jnp.zeros_like(acc_scr)

    tm, D = acc_scr.shape
    tn = wg_ref.shape[1]
    seq = jnp.where(pl.program_id(0) < n_ctx_tiles, SEQ, DEC_SEQ)
    pos = lax.broadcasted_iota(jnp.int32, (tm, FFN_SUB), 0) & (seq - 1)
    keep_prev = (pos != 0).astype(F32)
    keep_next = (pos != seq - 1).astype(F32)
    a = a_scr[...]

    def conv(u, cw_ref, cb_ref, cs):
        u_prev = pltpu.roll(u, 1, 0) * keep_prev
        u_next = pltpu.roll(u, tm - 1, 0) * keep_next
        return u_prev * cw_ref[0:1, cs] + u * cw_ref[1:2, cs] + u_next * cw_ref[2:3, cs] + cb_ref[:, cs]

    subs = [slice(s * FFN_SUB, (s + 1) * FFN_SUB) for s in range(tn // FFN_SUB)]
    ups = [(_dot(a, wg_ref[:, cs]), _dot(a, wv_ref[:, cs])) for cs in subs]
    for cs, (ug, uv) in zip(subs, ups):
        gate = conv(ug, cwg_ref, cbg_ref, cs)
        val = conv(uv, cwv_ref, cbv_ref, cs)
        act = (gate * _sigmoid(gate) * val).astype(BF16)
        for n in range(D // FFN_ACC_COLS):
            ns = slice(n * FFN_ACC_COLS, (n + 1) * FFN_ACC_COLS)
            acc_scr[:, ns] += _dot(act, wd_ref[cs, ns])

    @pl.when(j == pl.num_programs(1) - 1)
    def _():
        o_ref[...] = x_ref[...] + mod_ref[0, 5:6, :] * _rms(acc_scr[...], gpost_ref[...])


def _conv_ffn(x, g_pre, g_post, modg, w_up_bf16, w_down_bf16, l, conv_w, conv_b, n_ctx_rows):
    R, D = x.shape
    tm, tn = DEC_SEQ, 512
    nj = D_FF // tn
    cb = conv_b.reshape(1, 2 * D_FF)
    return pl.pallas_call(
        functools.partial(_ffn_kernel, n_ctx_tiles=n_ctx_rows // tm),
        grid=(R // tm, nj),
        in_specs=[pl.BlockSpec((tm, D), lambda i, j: (i, 0), pipeline_mode=pl.Buffered(1)),
                  pl.BlockSpec((1, D), lambda i, j: (0, 0)),
                  pl.BlockSpec((1, 6, D), lambda i, j: (i * (tm // ROW_GROUP), 0, 0)),
                  pl.BlockSpec((None, D, tn), lambda i, j: (l, 0, j)),
                  pl.BlockSpec((None, D, tn), lambda i, j: (l, 0, j + nj)),
                  pl.BlockSpec((3, tn), lambda i, j: (0, j)),
                  pl.BlockSpec((3, tn), lambda i, j: (0, j + nj)),
                  pl.BlockSpec((1, tn), lambda i, j: (0, j)),
                  pl.BlockSpec((1, tn), lambda i, j: (0, j + nj)),
                  pl.BlockSpec((None, tn, D), lambda i, j: (l, j, 0)),
                  pl.BlockSpec((1, D), lambda i, j: (0, 0))],
        out_specs=pl.BlockSpec((tm, D), lambda i, j: (i, 0), pipeline_mode=pl.Buffered(1)),
        out_shape=jax.ShapeDtypeStruct((R, D), F32),
        scratch_shapes=[pltpu.VMEM((tm, D), BF16), pltpu.VMEM((tm, D), F32)],
        compiler_params=_params(("parallel", "arbitrary"),
                                3 * tm * D * 4 + tm * D * 2 + 6 * D * tn * 2 + 16 * tm * FFN_SUB * 4),
        name="conv_ffn",
    )(x, g_pre.reshape(1, D), modg, w_up_bf16, w_up_bf16, conv_w, conv_w, cb, cb, w_down_bf16,
      g_post.reshape(1, D))


def _layout_w_in(w):
    o = np.cumsum([0, Q_LORA, KV_LORA, MLA_ROPE, FNET_WIDTH, 3 * RWKV_WIDTH, W_LORA, A_LORA, G_LORA])
    q_dn, kv_dn, k_rope, xf, rkv, w_lo, a_lo, g_lo = [w[..., o[i]:o[i + 1]] for i in range(8)]
    zeros = jnp.zeros(w.shape[:-1] + (LANE - MLA_ROPE,), w.dtype)
    tail = jnp.zeros(w.shape[:-1] + (IN_PAD - COL_GLO - G_LORA,), w.dtype)
    return jnp.concatenate([rkv, q_dn, xf, kv_dn, k_rope, zeros, w_lo, a_lo, g_lo, tail], axis=-1).astype(BF16)


def _layout_w_uq(w):
    L = w.shape[0]
    w = w.reshape(L, Q_LORA, MLA_HEADS, MLA_NOPE + MLA_ROPE)
    w = jnp.pad(w, ((0, 0), (0, 0), (0, 0), (0, HEAD_PAD - MLA_NOPE - MLA_ROPE)))
    return w.reshape(L, Q_LORA, MLA_HEADS * HEAD_PAD).astype(BF16)


def _layout_w_ukv(w):
    L = w.shape[0]
    w = w.reshape(L, KV_LORA, MLA_HEADS, MLA_NOPE + MLA_V)
    wk = w[..., :MLA_NOPE].reshape(L, KV_LORA, MLA_HEADS * MLA_NOPE)
    wv = w[..., MLA_NOPE:].reshape(L, KV_LORA, MLA_HEADS * MLA_V)
    return wk.astype(BF16), wv.astype(BF16)


def _rope_table(tm):
    t = jnp.arange(DEC_SEQ)
    rowp = (t // GRID_W).astype(F32)
    colp = (t % GRID_W).astype(F32)
    n = MLA_ROPE // 4
    inv = ROPE_BASE ** (-jnp.arange(n, dtype=F32) / n)
    ang = jnp.concatenate([rowp[:, None] * inv, colp[:, None] * inv], axis=-1)
    cos = jnp.concatenate([jnp.ones((tm, MLA_ROPE // 2), F32), jnp.cos(ang)], axis=0)
    sin = jnp.concatenate([jnp.zeros((tm, MLA_ROPE // 2), F32), jnp.sin(ang)], axis=0)
    z32 = jnp.zeros_like(cos)
    z64 = jnp.zeros((cos.shape[0], LANE - MLA_ROPE), F32)
    mul = jnp.concatenate([cos, cos, z64], axis=1)
    left = jnp.concatenate([-sin, z32, z64], axis=1)
    right = jnp.concatenate([z32, sin, z64], axis=1)
    return jnp.stack([mul, left, right])


def kernel(x_prompt, x_sample, cache_mla_ckv, cache_mla_krope, state_rwkv, c, c_ctx, w_mod, b_mod, g_pre_mix, g_post_mix, g_pre_ffn, g_post_ffn, w_in, g_q_norm, w_uq, g_kv_norm, w_ukv, rwkv_conv, rwkv_w0, rwkv_w2, rwkv_a0, rwkv_a2, rwkv_g2, rwkv_k_k, rwkv_k_a, rwkv_r_k, rwkv_gn_g, rwkv_gn_b, w_out, ffn_w_up, ffn_conv, ffn_conv_b, ffn_w_down):
    Bc, Tc, D = x_prompt.shape
    Bl, Tl, _ = x_sample.shape
    L = w_mod.shape[0]
    assert (Tc, Tl, D) == (SEQ, DEC_SEQ, D_MODEL) and cache_mla_ckv.shape[2] == PAST_LEN
    assert Bl + 1 <= 8 and (Bc * Tc) % DEC_SEQ == 0
    n_ctx = Bc * Tc
    n_lat = Bl * Tl
    x = jnp.concatenate([x_prompt.reshape(n_ctx, D), x_sample.reshape(n_lat, D)], axis=0)

    cond = jnp.zeros((8, D), F32).at[0].set(c_ctx).at[1:1 + Bl].set(c)
    mods = _modulation(cond, w_mod, b_mod)
    group_row = np.concatenate([np.zeros(n_ctx // ROW_GROUP, np.int32),
                                1 + np.arange(n_lat // ROW_GROUP, dtype=np.int32) // (Tl // ROW_GROUP)])

    tm_mla = 512
    rope_tab = _rope_table(tm_mla)
    ridx = _rope_index(tm_mla, n_ctx)
    ridx_cache = lambda i: 0
    zero_state = jnp.zeros((Bc, 2, RWKV_HEADS, RWKV_HEAD_DIM, RWKV_HEAD_DIM), F32)

    w_in_b = _layout_w_in(w_in)
    w_uq_b = _layout_w_uq(w_uq)
    wk_b, wv_b = _layout_w_ukv(w_ukv)
    w_out_b = w_out.astype(BF16)
    w_up_b = ffn_w_up.astype(BF16)
    w_down_b = ffn_w_down.astype(BF16)

    ckv_out, krope_out, state_out = [], [], []
    for l in range(L):
        modg = mods[l][group_row].reshape(-1, 6, D)
        p = dict(rwkv_conv=rwkv_conv[l], rwkv_w0=rwkv_w0[l], rwkv_w2=rwkv_w2[l], rwkv_a0=rwkv_a0[l],
                 rwkv_a2=rwkv_a2[l], rwkv_g2=rwkv_g2[l], rwkv_k_k=rwkv_k_k[l], rwkv_k_a=rwkv_k_a[l],
                 rwkv_r_k=rwkv_r_k[l])

        hin = _in_projection(x, g_pre_mix[l], modg, w_in_b, l)

        q = _q_projection(hin, g_q_norm[l], w_uq_b, l, rope_tab, n_ctx)
        ckv, k, v = _kv_projection(hin, COL_KVDN // KV_LORA, COL_KROPE // LANE, g_kv_norm[l], wk_b, wv_b, l,
                                   rope_tab, ridx, tm_mla, True)
        cache_src = jnp.concatenate(
            [cache_mla_ckv[:, l].reshape(Bl * PAST_LEN, KV_LORA),
             jnp.pad(cache_mla_krope[:, l].reshape(Bl * PAST_LEN, MLA_ROPE), ((0, 0), (0, LANE - MLA_ROPE)))],
            axis=1)
        k_c, v_c = _kv_projection(cache_src, 0, KV_LORA // LANE, g_kv_norm[l], wk_b, wv_b, l,
                                  rope_tab, ridx_cache, PAST_LEN, False)
        attn = (_attention(q, 0, Bc, Tc, [(k, v, 0, Tc)]),
                _attention(q, n_ctx, Bl, Tl, [(k_c, v_c, 0, PAST_LEN), (k, v, n_ctx, Tl)]))

        four = (_fourier_mix(hin, 0, Bc, Tc), _fourier_mix(hin, n_ctx, Bl, Tl))

        prep = _rwkv_prep(hin, p, n_ctx)
        yf_ctx, yb_ctx, s_ctx = _rwkv_scan(prep, 0, Bc, Tc, zero_state)
        yf_lat, yb_lat, _ = _rwkv_scan(prep, n_ctx, Bl, Tl, state_rwkv[:, l])
        rw = _rwkv_finalize(list(yf_ctx) + list(yf_lat), list(yb_ctx) + list(yb_lat), prep[4], prep[3],
                            rwkv_gn_g[l], rwkv_gn_b[l])

        x = _out_projection(attn, four, rw, w_out_b, l, x, g_post_mix[l], modg)
        x = _conv_ffn(x, g_pre_ffn[l], g_post_ffn[l], modg, w_up_b, w_down_b, l, ffn_conv[l], ffn_conv_b[l], n_ctx)

        ckv_out.append(ckv[:n_ctx].reshape(Bc, Tc, KV_LORA))
        krope_out.append(hin[:n_ctx, COL_KROPE:COL_KROPE + MLA_ROPE].reshape(Bc, Tc, MLA_ROPE))
        state_out.append(s_ctx)

    return (x[:n_ctx].reshape(Bc, Tc, D), x[n_ctx:].reshape(Bl, Tl, D),
            jnp.stack(ckv_out, axis=1), jnp.stack(krope_out, axis=1), jnp.stack(state_out, axis=1))
```

```python
import functools
import math

import jax
import jax.numpy as jnp
import numpy as np
from jax import lax
from jax.experimental import pallas as pl
from jax.experimental.pallas import tpu as pltpu

F32 = jnp.float32
BF16 = jnp.bfloat16

D_MODEL = 2048
SEQ = 256
DEC_SEQ = 1024
PAST_LEN = 256
GRID_W = 64
MLA_HEADS = 8
MLA_NOPE = 128
MLA_ROPE = 64
MLA_V = 128
Q_LORA = 512
KV_LORA = 256
FNET_GROUPS = 4
FNET_GROUP_DIM = 128
FNET_WIDTH = FNET_GROUPS * FNET_GROUP_DIM
RWKV_HEADS = 8
RWKV_HEAD_DIM = 64
RWKV_WIDTH = RWKV_HEADS * RWKV_HEAD_DIM
W_LORA = 64
A_LORA = 64
G_LORA = 128
D_FF = 5632
ROPE_BASE = 10000.0
EPS = 1e-6
GN_EPS = 64e-5
DECAY_SCALE = math.exp(-0.5)

LANE = 128
ROW_GROUP = 256
HEAD_PAD = 256
CHUNK = 64
VMEM_CAP = 60 * 1024 * 1024

COL_RKV = 0
COL_QDN = 3 * RWKV_WIDTH
COL_XF = COL_QDN + Q_LORA
COL_KVDN = COL_XF + FNET_WIDTH
COL_KROPE = COL_KVDN + KV_LORA
COL_WA = COL_KROPE + LANE
COL_GLO = COL_WA + LANE
MXU_COLS = 256
IN_PAD = -(-(COL_GLO + G_LORA) // MXU_COLS) * MXU_COLS


SMALL_OPERANDS = 8 * 1024 * 1024


def _params(sem, nbytes):
    return pltpu.CompilerParams(dimension_semantics=sem,
                                vmem_limit_bytes=int(min(VMEM_CAP, nbytes + SMALL_OPERANDS)))


def _dot(a, b):
    return jnp.dot(a, b, preferred_element_type=F32)


def _dot_nt(a, b):
    return lax.dot_general(a, b, (((1,), (1,)), ((), ())), preferred_element_type=F32)


def _dot_tn(a, b):
    return lax.dot_general(a, b, (((0,), (0,)), ((), ())), preferred_element_type=F32)


def _split_bf16(x, n):
    terms = []
    rem = x
    for i in range(n):
        t = rem.astype(BF16)
        terms.append(t)
        if i + 1 < n:
            rem = rem - t.astype(F32)
    return terms


def _dot_exact_rhs(a, b_exact_bf16, n):
    acc = None
    for t in _split_bf16(a, n):
        p = _dot(t, b_exact_bf16)
        acc = p if acc is None else acc + p
    return acc


def _rms(x, g):
    return x * lax.rsqrt(jnp.mean(x * x, axis=-1, keepdims=True) + EPS) * g


def _sigmoid(x):
    return 1.0 / (1.0 + jnp.exp(-x))


def _mod_kernel(cond_ref, w_ref, b_ref, o_ref):
    c = cond_ref[...]
    s = (c * _sigmoid(c)).astype(BF16)
    o_ref[0] = _dot(s, w_ref[0].astype(BF16)) + b_ref[0]


def _modulation(cond, w_mod, b_mod):
    L, D, N = w_mod.shape
    tn = 1024
    return pl.pallas_call(
        _mod_kernel,
        grid=(L, N // tn),
        in_specs=[pl.BlockSpec((8, D), lambda l, j: (0, 0)),
                  pl.BlockSpec((1, D, tn), lambda l, j: (l, 0, j)),
                  pl.BlockSpec((1, 1, tn), lambda l, j: (l, 0, j))],
        out_specs=pl.BlockSpec((1, 8, tn), lambda l, j: (l, 0, j)),
        out_shape=jax.ShapeDtypeStruct((L, 8, N), F32),
        compiler_params=_params(("parallel", "parallel"), 3 * D * tn * 4),
        name="modulation",
    )(cond, w_mod, b_mod.reshape(L, 1, N))


def _inproj_kernel(x_ref, g_ref, mod_ref, w_ref, o_ref):
    h = _rms(x_ref[...], g_ref[...]) * (1.0 + mod_ref[0, 1:2, :]) + mod_ref[0, 0:1, :]
    o_ref[...] = _dot(h.astype(BF16), w_ref[...])


def _in_projection(x, g, modg, w_bf16, l):
    R, D = x.shape
    N = w_bf16.shape[2]
    tm = 512
    return pl.pallas_call(
        _inproj_kernel,
        grid=(R // tm,),
        in_specs=[pl.BlockSpec((tm, D), lambda i: (i, 0)),
                  pl.BlockSpec((1, D), lambda i: (0, 0)),
                  pl.BlockSpec((1, 6, D), lambda i: (i * (tm // ROW_GROUP), 0, 0)),
                  pl.BlockSpec((None, D, N), lambda i: (l, 0, 0), pipeline_mode=pl.Buffered(1))],
        out_specs=pl.BlockSpec((tm, N), lambda i: (i, 0)),
        out_shape=jax.ShapeDtypeStruct((R, N), F32),
        compiler_params=_params(("parallel",), 2 * tm * D * 4 + tm * D * 2 + D * N * 2 + 3 * tm * N * 4),
        name="in_projection",
    )(x, g.reshape(1, D), modg, w_bf16)


def _rope_block(blk, rope_ref):
    return (blk * rope_ref[0]
            + pltpu.roll(blk, LANE - MLA_ROPE // 2, 1) * rope_ref[1]
            + pltpu.roll(blk, MLA_ROPE // 2, 1) * rope_ref[2])


def _q_kernel(qdn_ref, g_ref, w_ref, rope_ref, o_ref):
    scale = (MLA_NOPE + MLA_ROPE) ** -0.5
    qn = _rms(qdn_ref[...], g_ref[...]).astype(BF16)
    q = _dot(qn, w_ref[...]) * scale
    parts = []
    for h in range(MLA_HEADS):
        base = h * HEAD_PAD
        parts.append(q[:, base:base + MLA_NOPE])
        parts.append(_rope_block(q[:, base + MLA_NOPE:base + HEAD_PAD], rope_ref))
    o_ref[...] = jnp.concatenate(parts, axis=1).astype(o_ref.dtype)


def _rope_index(tm, n_ctx_rows):
    n_ctx_tiles = n_ctx_rows // tm
    per_seq = DEC_SEQ // tm

    def index(i):
        return jnp.where(i < n_ctx_tiles, 0, 1 + (i - n_ctx_tiles) % per_seq)
    return index


def _q_projection(hin, g, w_bf16, l, rope_tab, n_ctx_rows):
    R = hin.shape[0]
    tm = 512
    N = MLA_HEADS * HEAD_PAD
    ridx = _rope_index(tm, n_ctx_rows)
    return pl.pallas_call(
        _q_kernel,
        grid=(R // tm,),
        in_specs=[pl.BlockSpec((tm, Q_LORA), lambda i: (i, COL_QDN // Q_LORA)),
                  pl.BlockSpec((1, Q_LORA), lambda i: (0, 0)),
                  pl.BlockSpec((None, Q_LORA, N), lambda i: (l, 0, 0)),
                  pl.BlockSpec((3, tm, LANE), lambda i: (0, ridx(i), 0))],
        out_specs=pl.BlockSpec((tm, N), lambda i: (i, 0)),
        out_shape=jax.ShapeDtypeStruct((R, N), BF16),
        compiler_params=_params(("parallel",), 2 * Q_LORA * N * 2 + 6 * tm * N * 4),
        name="q_projection",
    )(hin, g.reshape(1, Q_LORA), w_bf16, rope_tab)


def _kv_kernel(kvdn_ref, kr_ref, g_ref, wk_ref, wv_ref, rope_ref, *out_refs, normalize):
    k_ref, v_ref = out_refs[-2:]
    c = kvdn_ref[...]
    if normalize:
        c = _rms(c, g_ref[...])
        out_refs[0][...] = c
    cb = c.astype(BF16)
    kn = _dot(cb, wk_ref[...])
    v_ref[...] = _dot(cb, wv_ref[...]).astype(v_ref.dtype)
    kr = _rope_block(kr_ref[...], rope_ref)
    parts = []
    for h in range(MLA_HEADS):
        parts.append(kn[:, h * MLA_NOPE:(h + 1) * MLA_NOPE])
        parts.append(kr)
    k_ref[...] = jnp.concatenate(parts, axis=1).astype(k_ref.dtype)


def _kv_projection(src, kv_col, kr_col, g, wk_bf16, wv_bf16, l, rope_tab, ridx, tm, normalize):
    R = src.shape[0]
    NK = MLA_HEADS * HEAD_PAD
    NV = MLA_HEADS * MLA_V
    return pl.pallas_call(
        functools.partial(_kv_kernel, normalize=normalize),
        grid=(R // tm,),
        in_specs=[pl.BlockSpec((tm, KV_LORA), lambda i: (i, kv_col)),
                  pl.BlockSpec((tm, LANE), lambda i: (i, kr_col)),
                  pl.BlockSpec((1, KV_LORA), lambda i: (0, 0)),
                  pl.BlockSpec((None, KV_LORA, MLA_HEADS * MLA_NOPE), lambda i: (l, 0, 0)),
                  pl.BlockSpec((None, KV_LORA, NV), lambda i: (l, 0, 0)),
                  pl.BlockSpec((3, tm, LANE), lambda i: (0, ridx(i), 0))],
        out_specs=([pl.BlockSpec((tm, KV_LORA), lambda i: (i, 0))] if normalize else [])
                  + [pl.BlockSpec((tm, NK), lambda i: (i, 0)), pl.BlockSpec((tm, NV), lambda i: (i, 0))],
        out_shape=([jax.ShapeDtypeStruct((R, KV_LORA), F32)] if normalize else [])
                  + [jax.ShapeDtypeStruct((R, NK), BF16), jax.ShapeDtypeStruct((R, NV), BF16)],
        compiler_params=_params(("parallel",), 8 * tm * NK * 4),
        name="kv_projection" if normalize else "kv_projection_cache",
    )(src, src, g.reshape(1, KV_LORA), wk_bf16, wv_bf16, rope_tab)


def _attn_kernel(q_ref, *refs):
    o_ref = refs[-1]
    kv = [(refs[i], refs[i + 1]) for i in range(0, len(refs) - 1, 2)]
    for h in range(MLA_HEADS):
        qk = slice(h * HEAD_PAD, (h + 1) * HEAD_PAD)
        vo = slice(h * MLA_V, (h + 1) * MLA_V)
        q = q_ref[:, qk]
        scores = [_dot_nt(q, k_ref[:, qk]) for k_ref, _ in kv]
        m = functools.reduce(jnp.maximum, [jnp.max(s, axis=-1, keepdims=True) for s in scores])
        probs = [jnp.exp(s - m) for s in scores]
        l = functools.reduce(jnp.add, [jnp.sum(p, axis=-1, keepdims=True) for p in probs])
        o = functools.reduce(jnp.add, [_dot(p.astype(BF16), v_ref[:, vo]) for p, (_, v_ref) in zip(probs, kv)])
        o_ref[:, vo] = (o / l).astype(o_ref.dtype)


def _attention(q, q_row0, n_batch, tq_total, sources):
    tq = min(512, tq_total)
    nq = tq_total // tq
    q0 = q_row0 // tq
    NQ = MLA_HEADS * HEAD_PAD
    NV = MLA_HEADS * MLA_V
    in_specs = [pl.BlockSpec((tq, NQ), lambda b, i: (q0 + b * nq + i, 0))]
    args = [q]
    tk_total = 0
    for k, v, row0, tk in sources:
        in_specs.append(pl.BlockSpec((tk, NQ), lambda b, i, o=row0 // tk: (o + b, 0)))
        in_specs.append(pl.BlockSpec((tk, NV), lambda b, i, o=row0 // tk: (o + b, 0)))
        args += [k, v]
        tk_total += tk
    return pl.pallas_call(
        _attn_kernel,
        grid=(n_batch, nq),
        in_specs=in_specs,
        out_specs=pl.BlockSpec((tq, NV), lambda b, i: (b * nq + i, 0)),
        out_shape=jax.ShapeDtypeStruct((n_batch * tq_total, NV), BF16),
        compiler_params=_params(("parallel", "arbitrary"),
                                2 * (tq + tk_total) * (NQ + NV) * 2 + 6 * tq * tk_total * 4),
        name="attention",
    )(*args)


def _fnet_kernel(z_ref, cs_ref, ct_ref, o_ref):
    yc, ys = [], []
    for g in range(FNET_GROUPS):
        zg = z_ref[:, g * FNET_GROUP_DIM:(g + 1) * FNET_GROUP_DIM].astype(BF16)
        y = _dot(zg, cs_ref[...])
        yc.append(y[:, :FNET_GROUP_DIM])
        ys.append(y[:, FNET_GROUP_DIM:])
    stacked = jnp.concatenate([jnp.concatenate(yc, axis=1), jnp.concatenate(ys, axis=1)], axis=0)
    o_ref[...] = _dot(ct_ref[...], stacked.astype(BF16)).astype(o_ref.dtype)


def _dft_tables(T):
    def cs(n):
        i = jnp.arange(n, dtype=jnp.int32)
        ang = ((i[:, None] * i[None, :]) % n).astype(F32) * (2.0 * math.pi / n)
        return jnp.cos(ang), jnp.sin(ang)
    cc, sc = cs(FNET_GROUP_DIM)
    ct, st = cs(T)
    norm = 1.0 / math.sqrt(T * FNET_GROUP_DIM)
    return (jnp.concatenate([cc, sc], axis=1).astype(BF16),
            (jnp.concatenate([ct, -st], axis=1) * norm).astype(BF16))


def _fourier_mix(hin, row0, n_batch, T):
    cs_tab, ct_tab = _dft_tables(T)
    b0 = row0 // T
    return pl.pallas_call(
        _fnet_kernel,
        grid=(n_batch,),
        in_specs=[pl.BlockSpec((T, FNET_WIDTH), lambda b: (b0 + b, COL_XF // FNET_WIDTH)),
                  pl.BlockSpec((FNET_GROUP_DIM, 2 * FNET_GROUP_DIM), lambda b: (0, 0)),
                  pl.BlockSpec((T, 2 * T), lambda b: (0, 0))],
        out_specs=pl.BlockSpec((T, FNET_WIDTH), lambda b: (b, 0)),
        out_shape=jax.ShapeDtypeStruct((n_batch * T, FNET_WIDTH), BF16),
        compiler_params=_params(("parallel",), 2 * T * 2 * T * 2 + 10 * T * FNET_WIDTH * 4),
        name="fourier_mix",
    )(hin, cs_tab, ct_tab)


def _head_sum_matrix(scale):
    i = np.arange(RWKV_WIDTH) // RWKV_HEAD_DIM
    return jnp.asarray((i[:, None] == i[None, :]).astype(np.float32) * scale, BF16)


def _rwkv_prep_kernel(x_ref, xp_ref, xn_ref, wa_ref, gl_ref, cw_ref, w0_ref, w2_ref, a0_ref, a2_ref, g2_ref,
                      kk_ref, ka_ref, rk_ref, hs_ref,
                      r_ref, kh_ref, v_ref, g_ref, bonus_ref, lw_ref, kt_ref, beta_ref,
                      *, n_ctx_tiles, tiles_per_seq):
    i = pl.program_id(0)
    j = i - n_ctx_tiles
    first = jnp.logical_or(i < n_ctx_tiles, j % tiles_per_seq == 0)
    last = jnp.logical_or(i < n_ctx_tiles, j % tiles_per_seq == tiles_per_seq - 1)
    x = x_ref[...]
    tm = x.shape[0]
    row = lax.broadcasted_iota(jnp.int32, x.shape, 0)
    prev_row = jnp.where(first, 0.0, xp_ref[7:8, :])
    next_row = jnp.where(last, 0.0, xn_ref[0:1, :])
    x_prev = jnp.where(row == 0, prev_row, pltpu.roll(x, 1, 0))
    x_next = jnp.where(row == tm - 1, next_row, pltpu.roll(x, tm - 1, 0))
    xc = x_prev * cw_ref[0:1, :] + x * cw_ref[1:2, :] + x_next * cw_ref[2:3, :]
    R_ = RWKV_WIDTH
    r, k, v = xc[:, :R_], xc[:, R_:2 * R_], xc[:, 2 * R_:]
    wl = jnp.tanh(wa_ref[:, :W_LORA]).astype(BF16)
    al = wa_ref[:, W_LORA:W_LORA + A_LORA].astype(BF16)
    g_ref[...] = _dot(_sigmoid(gl_ref[...]).astype(BF16), g2_ref[...].astype(BF16))
    kappa = k * kk_ref[...]
    ss = _dot_exact_rhs(kappa * kappa, hs_ref[...], 3)
    kh = kappa * lax.rsqrt(ss + EPS)
    kt_sum = None
    for d in range(2):
        lw_ref[d] = -DECAY_SCALE * _sigmoid(w0_ref[d:d + 1, :] + _dot(wl, w2_ref[d].astype(BF16)))
        a = _sigmoid(a0_ref[d:d + 1, :] + _dot(al, a2_ref[d].astype(BF16)))
        kt = k * (1.0 + (a - 1.0) * ka_ref[...])
        kt_ref[d] = kt
        beta_ref[d] = a * kh
        kt_sum = kt if kt_sum is None else kt_sum + kt
    r_ref[...] = r
    kh_ref[...] = kh
    v_ref[...] = v
    bonus_ref[...] = _dot_exact_rhs(r * kt_sum * rk_ref[...], hs_ref[...], 3) * v


def _rwkv_prep(hin, p, n_ctx_rows):
    R = hin.shape[0]
    tm = ROW_GROUP
    W3 = 3 * RWKV_WIDTH
    nh = tm // 8
    last_blk = R // 8 - 1
    row = lambda a: a.reshape(1, RWKV_WIDTH)
    full = lambda *s: pl.BlockSpec(s, lambda i: (0,) * len(s))
    out_rw = jax.ShapeDtypeStruct((R, RWKV_WIDTH), F32)
    out_rw2 = jax.ShapeDtypeStruct((2, R, RWKV_WIDTH), F32)
    spec_rw = pl.BlockSpec((tm, RWKV_WIDTH), lambda i: (i, 0))
    spec_rw2 = pl.BlockSpec((2, tm, RWKV_WIDTH), lambda i: (0, i, 0))
    return pl.pallas_call(
        functools.partial(_rwkv_prep_kernel, n_ctx_tiles=n_ctx_rows // tm, tiles_per_seq=DEC_SEQ // tm),
        grid=(R // tm,),
        in_specs=[pl.BlockSpec((tm, W3), lambda i: (i, COL_RKV // W3)),
                  pl.BlockSpec((8, W3), lambda i: (jnp.maximum(i * nh - 1, 0), 0)),
                  pl.BlockSpec((8, W3), lambda i: (jnp.minimum((i + 1) * nh, last_blk), 0)),
                  pl.BlockSpec((tm, LANE), lambda i: (i, COL_WA // LANE)),
                  pl.BlockSpec((tm, G_LORA), lambda i: (i, COL_GLO // G_LORA)),
                  full(3, W3), full(2, RWKV_WIDTH), full(2, W_LORA, RWKV_WIDTH),
                  full(2, RWKV_WIDTH), full(2, A_LORA, RWKV_WIDTH), full(G_LORA, RWKV_WIDTH),
                  full(1, RWKV_WIDTH), full(1, RWKV_WIDTH), full(1, RWKV_WIDTH),
                  full(RWKV_WIDTH, RWKV_WIDTH)],
        out_specs=[spec_rw] * 5 + [spec_rw2] * 3,
        out_shape=[out_rw] * 5 + [out_rw2] * 3,
        compiler_params=_params(("parallel",), 40 * tm * W3 * 4),
        name="rwkv_prep",
    )(hin, hin, hin, hin, hin, p['rwkv_conv'], p['rwkv_w0'], p['rwkv_w2'], p['rwkv_a0'], p['rwkv_a2'],
      p['rwkv_g2'], row(p['rwkv_k_k']), row(p['rwkv_k_a']), row(p['rwkv_r_k']), _head_sum_matrix(1.0))


def _rwkv_scan_kernel(*refs, n_seq):
    seq_refs = [refs[13 * p:13 * (p + 1)] for p in range(n_seq)]
    mi_ref = refs[13 * n_seq]
    out_refs = refs[13 * n_seq + 1:-1]
    s_scr = refs[-1]
    c = pl.program_id(1)
    nc = pl.num_programs(1)
    N = RWKV_HEAD_DIM
    C = CHUNK
    H = RWKV_HEADS

    @pl.when(c == 0)
    def _():
        for p in range(n_seq):
            s_scr[p] = seq_refs[p][12][0]

    eye = (lax.broadcasted_iota(jnp.int32, (C, C), 0) == lax.broadcasted_iota(jnp.int32, (C, C), 1)).astype(F32)
    dirs = [(p, d, tuple(sr[3 * d:3 * d + 3]) + tuple(sr[6 + 3 * d:9 + 3 * d]))
            for p, sr in enumerate(seq_refs) for d in range(2)]
    chains = []
    for p, d, (r_ref, kh_ref, v_ref, lw_ref, kt_ref, beta_ref) in dirs:
        mi = mi_ref[d]
        lw = lw_ref[0]
        cl = _dot_exact_lhs(mi.astype(BF16), lw, 3)
        cl_tot = jnp.sum(lw, axis=0, keepdims=True)
        g_inv = jnp.exp(-cl)
        g_end = jnp.exp(cl_tot - cl)
        g_tot = jnp.exp(cl_tot)
        kh, beta, kt = kh_ref[...], beta_ref[0], kt_ref[0]
        k_dec = kh * jnp.exp(cl - lw)
        r_dec = r_ref[...] * jnp.exp(cl)
        b_inv, k_inv = beta * g_inv, kt * g_inv
        k_end, b_end = kt * g_end, beta * g_end
        v_all = v_ref[...]
        for h in range(H):
            sl = slice(h * N, (h + 1) * N)
            chains.append(dict(p=p, d=d, h=h, mi=mi, ms=mi - eye, kd=k_dec[:, sl], rd=r_dec[:, sl],
                               bi=b_inv[:, sl], ki=k_inv[:, sl], ke=k_end[:, sl], be=b_end[:, sl],
                               v=v_all[:, sl], gt=g_tot[:, sl], s0=s_scr[p, d, h]))

    for ch in chains:
        lhs = jnp.concatenate([ch['kd'], ch['rd']], axis=0).astype(BF16)
        rhs = jnp.concatenate([ch['bi'], ch['ki']], axis=0).astype(BF16)
        nt = _dot_nt(lhs, rhs)
        ch['a'] = nt[:C, :C] * ch['ms']
        ch['b'] = (nt[:C, C:] * ch['ms']).astype(BF16)
        ch['qp'] = (nt[C:, :] * jnp.concatenate([-ch['mi'], ch['mi']], axis=1)).astype(BF16)
        ch['lhs'] = lhs
        ch['t'] = eye - ch['a']
    for ch in chains:
        xb = (-ch['a']).astype(BF16)
        ch['xp'] = _dot(xb, xb).astype(BF16)
    n_sq = int(math.log2(C)) - 1
    for k in range(n_sq):
        last = k == n_sq - 1
        for ch in chains:
            tb = ch['t'].astype(BF16)
            prod = _dot(tb if last else jnp.concatenate([tb, ch['xp']], axis=0), ch['xp'])
            ch['t'] = ch['t'] + prod[:C]
            if not last:
                ch['xp'] = prod[C:].astype(BF16)
    for ch in chains:
        ch['s0b'] = ch['s0'].astype(BF16)
        ch['vb'] = ch['v'].astype(BF16)
        ks = _dot_nt(ch['lhs'], ch['s0b'])
        ch['rs'] = ks[C:]
        ch['z'] = ks[:C] + _dot(ch['b'], ch['vb'])
    for ch in chains:
        u = _dot(ch['t'].astype(BF16), ch['z'].astype(BF16))
        ch['uv'] = jnp.concatenate([u.astype(BF16), ch['vb']], axis=0)
    for ch in chains:
        kb = jnp.concatenate([-ch['be'], ch['ke']], axis=0).astype(BF16)
        s_scr[ch['p'], ch['d'], ch['h']] = ch['s0'] * ch['gt'] + _dot_tn(ch['uv'], kb)
    for ch in chains:
        ch['y'] = ch['rs'] + _dot(ch['qp'], ch['uv'])
    for p in range(n_seq):
        for d in range(2):
            ys = [ch['y'] for ch in chains[(2 * p + d) * H:(2 * p + d + 1) * H]]
            out_refs[3 * p + d][...] = jnp.concatenate(ys, axis=1)

    @pl.when(c == nc - 1)
    def _():
        for p in range(n_seq):
            out_refs[3 * p + 2][0] = s_scr[p]


def _dot_exact_lhs(a_exact_bf16, b, n):
    acc = None
    for t in _split_bf16(b, n):
        p = _dot(a_exact_bf16, t)
        acc = p if acc is None else acc + p
    return acc


def _causal_masks():
    t = np.arange(CHUNK)
    fwd = (t[None, :] <= t[:, None]).astype(np.float32)
    return jnp.asarray(np.stack([fwd, fwd.T]))


def _rwkv_scan(prep, row0, n_batch, T, s0):
    r, kh, v, _, _, lw, kt, beta = prep
    C = CHUNK
    nc = T // C
    c0 = row0 // C
    W = RWKV_WIDTH
    n_seq = 2 if n_batch % 2 == 0 else 1
    nb = n_batch // n_seq
    st_shape = (1, 2, RWKV_HEADS, RWKV_HEAD_DIM, RWKV_HEAD_DIM)
    in_specs, args, out_specs, out_shape = [], [], [], []
    for p in range(n_seq):
        fwd = lambda b, c, p=p: (p * nb + b) * nc + c
        bwd = lambda b, c, p=p: (p * nb + b) * nc + nc - 1 - c
        shared_f = pl.BlockSpec((C, W), lambda b, c, f=fwd: (c0 + f(b, c), 0))
        shared_b = pl.BlockSpec((C, W), lambda b, c, f=bwd: (c0 + f(b, c), 0))
        dir_f = pl.BlockSpec((1, C, W), lambda b, c, f=fwd: (0, c0 + f(b, c), 0))
        dir_b = pl.BlockSpec((1, C, W), lambda b, c, f=bwd: (1, c0 + f(b, c), 0))
        in_specs += [shared_f] * 3 + [shared_b] * 3 + [dir_f] * 3 + [dir_b] * 3
        in_specs.append(pl.BlockSpec(st_shape, lambda b, c, p=p: (p * nb + b, 0, 0, 0, 0)))
        args += [r, kh, v, r, kh, v, lw, kt, beta, lw, kt, beta, s0]
        out_specs += [pl.BlockSpec((C, W), lambda b, c: (b * nc + c, 0)),
                      pl.BlockSpec((C, W), lambda b, c: (b * nc + nc - 1 - c, 0)),
                      pl.BlockSpec(st_shape, lambda b, c: (b, 0, 0, 0, 0))]
        out_shape += [jax.ShapeDtypeStruct((nb * T, W), F32)] * 2
        out_shape.append(jax.ShapeDtypeStruct((nb,) + st_shape[1:], F32))
    in_specs.append(pl.BlockSpec((2, C, C), lambda b, c: (0, 0, 0)))
    outs = pl.pallas_call(
        functools.partial(_rwkv_scan_kernel, n_seq=n_seq),
        grid=(nb, nc),
        in_specs=in_specs,
        out_specs=out_specs,
        out_shape=out_shape,
        scratch_shapes=[pltpu.VMEM((n_seq, 2, RWKV_HEADS, RWKV_HEAD_DIM, RWKV_HEAD_DIM), F32)],
        compiler_params=_params(("parallel", "arbitrary"), 32 * 1024 * 1024),
        name="rwkv_scan",
    )(*args, _causal_masks())
    return outs[0::3], outs[1::3], jnp.concatenate(outs[2::3], axis=0)


def _part_tiles(parts, tm):
    return [int(t) for t in np.cumsum([0] + [p.shape[0] // tm for p in parts])]


def _part_specs(parts, tm):
    starts = _part_tiles(parts, tm)
    return [pl.BlockSpec((tm, p.shape[1]), lambda i, t0=t0, n=t1 - t0: (jnp.clip(i - t0, 0, n - 1), 0))
            for p, t0, t1 in zip(parts, starts[:-1], starts[1:])]


def _for_own_part(starts, run):
    i = pl.program_id(0)
    for k, (t0, t1) in enumerate(zip(starts[:-1], starts[1:])):
        pl.when(jnp.logical_and(i >= t0, i < t1))(functools.partial(run, k))


def _rwkv_fin_kernel(*refs, starts):
    n = len(starts) - 1
    yf_refs, yb_refs = refs[:n], refs[n:2 * n]
    bonus_ref, g_ref, gg_ref, gb_ref, hm_ref, o_ref = refs[2 * n:]

    def run(k):
        y = yf_refs[k][...] + yb_refs[k][...]
        mu = _dot_exact_rhs(y, hm_ref[...], 3)
        yc = y - mu
        var = _dot_exact_rhs(yc * yc, hm_ref[...], 3)
        yn = yc * lax.rsqrt(var + GN_EPS) * gg_ref[...] + gb_ref[...]
        o_ref[...] = ((yn + bonus_ref[...]) * g_ref[...]).astype(o_ref.dtype)

    _for_own_part(starts, run)


def _rwkv_finalize(yf_parts, yb_parts, bonus, g, gn_g, gn_b):
    tm = 512
    W = RWKV_WIDTH
    starts = _part_tiles(yf_parts, tm)
    rows = pl.BlockSpec((tm, W), lambda i: (i, 0))
    vec = pl.BlockSpec((1, W), lambda i: (0, 0))
    return pl.pallas_call(
        functools.partial(_rwkv_fin_kernel, starts=starts),
        grid=(starts[-1],),
        in_specs=_part_specs(yf_parts, tm) + _part_specs(yb_parts, tm)
                 + [rows, rows, vec, vec, pl.BlockSpec((W, W), lambda i: (0, 0))],
        out_specs=rows,
        out_shape=jax.ShapeDtypeStruct((starts[-1] * tm, W), BF16),
        compiler_params=_params(("parallel",), 24 * tm * W * 4),
        name="rwkv_finalize",
    )(*yf_parts, *yb_parts, bonus, g, gn_g.reshape(1, W), gn_b.reshape(1, W),
      _head_sum_matrix(1.0 / RWKV_HEAD_DIM))


def _outproj_kernel(ac_ref, al_ref, fc_ref, fl_ref, rw_ref, w_ref, x_ref, g_ref, mod_ref, o_ref, *, starts):
    a0 = MLA_HEADS * MLA_V
    a1 = a0 + FNET_WIDTH

    def run(k):
        attn_ref, four_ref = ((ac_ref, fc_ref), (al_ref, fl_ref))[k]
        acc = _dot(attn_ref[...], w_ref[0:a0, :])
        acc += _dot(four_ref[...], w_ref[a0:a1, :])
        acc += _dot(rw_ref[...], w_ref[a1:, :])
        o_ref[...] = x_ref[...] + mod_ref[0, 2:3, :] * _rms(acc, g_ref[...])

    _for_own_part(starts, run)


def _out_projection(attn_parts, four_parts, rw, w_bf16, l, x, g, modg):
    R, D = x.shape
    tm = 256
    K = w_bf16.shape[1]
    return pl.pallas_call(
        functools.partial(_outproj_kernel, starts=_part_tiles(attn_parts, tm)),
        grid=(R // tm,),
        in_specs=_part_specs(attn_parts, tm) + _part_specs(four_parts, tm)
                 + [pl.BlockSpec((tm, rw.shape[1]), lambda i: (i, 0)),
                  pl.BlockSpec((None, K, D), lambda i: (l, 0, 0)),
                  pl.BlockSpec((tm, D), lambda i: (i, 0)),
                  pl.BlockSpec((1, D), lambda i: (0, 0)),
                  pl.BlockSpec((1, 6, D), lambda i: (i * (tm // ROW_GROUP), 0, 0))],
        out_specs=pl.BlockSpec((tm, D), lambda i: (i, 0)),
        out_shape=jax.ShapeDtypeStruct((R, D), F32),
        compiler_params=_params(("parallel",), 2 * K * D * 2 + 8 * tm * D * 4),
        name="out_projection",
    )(attn_parts[0], attn_parts[1], four_parts[0], four_parts[1], rw, w_bf16, x, g.reshape(1, D), modg)


FFN_SUB = 256
FFN_ACC_COLS = 512


def _ffn_kernel(x_ref, gpre_ref, mod_ref, wg_ref, wv_ref, cwg_ref, cwv_ref, cbg_ref, cbv_ref, wd_ref, gpost_ref,
                o_ref, a_scr, *, is_ctx_tile):
    j = pl.program_id(1)
    acc_scr = o_ref

    @pl.when(j == 0)
    def _():
        h = _rms(x_ref[...], gpre_ref[...]) * (1.0 + mod_ref[0, 4:5, :]) + mod_ref[0, 3:4, :]
        a_scr[...] = h.astype(BF16)
        acc_scr[...] = jnp.zeros_like(acc_scr)

    tm, D = acc_scr.shape
    tn = wg_ref.shape[1]
    seq = jnp.where(is_ctx_tile(pl.program_id(0)), SEQ, DEC_SEQ)
    pos = lax.broadcasted_iota(jnp.int32, (tm, FFN_SUB), 0) & (seq - 1)
    keep_prev = (pos != 0).astype(F32)
    keep_next = (pos != seq - 1).astype(F32)
    a = a_scr[...]

    def conv(u, cw_ref, cb_ref, cs):
        u_prev = pltpu.roll(u, 1, 0) * keep_prev
        u_next = pltpu.roll(u, tm - 1, 0) * keep_next
        return u_prev * cw_ref[0:1, cs] + u * cw_ref[1:2, cs] + u_next * cw_ref[2:3, cs] + cb_ref[:, cs]

    subs = [slice(s * FFN_SUB, (s + 1) * FFN_SUB) for s in range(tn // FFN_SUB)]
    ups = [(_dot(a, wg_ref[:, cs]), _dot(a, wv_ref[:, cs])) for cs in subs]
    for cs, (ug, uv) in zip(subs, ups):
        gate = conv(ug, cwg_ref, cbg_ref, cs)
        val = conv(uv, cwv_ref, cbv_ref, cs)
        act = (gate * _sigmoid(gate) * val).astype(BF16)
        for n in range(D // FFN_ACC_COLS):
            ns = slice(n * FFN_ACC_COLS, (n + 1) * FFN_ACC_COLS)
            acc_scr[:, ns] += _dot(act, wd_ref[cs, ns])

    @pl.when(j == pl.num_programs(1) - 1)
    def _():
        o_ref[...] = x_ref[...] + mod_ref[0, 5:6, :] * _rms(acc_scr[...], gpost_ref[...])


def _conv_ffn(x, g_pre, g_post, modg, w_up_bf16, w_down_bf16, l, conv_w, conv_b, n_ctx_rows, row0, n_rows):
    D = x.shape[1]
    tm, tn = DEC_SEQ, 2 * FFN_SUB
    nj = D_FF // tn
    t0 = row0 // tm
    n_ctx_tiles = n_ctx_rows // tm
    cb = conv_b.reshape(1, 2 * D_FF)
    return pl.pallas_call(
        functools.partial(_ffn_kernel, is_ctx_tile=lambda i: i + t0 < n_ctx_tiles),
        grid=(n_rows // tm, nj),
        in_specs=[pl.BlockSpec((tm, D), lambda i, j: (t0 + i, 0)),
                  pl.BlockSpec((1, D), lambda i, j: (0, 0)),
                  pl.BlockSpec((1, 6, D), lambda i, j: ((t0 + i) * (tm // ROW_GROUP), 0, 0)),
                  pl.BlockSpec((None, D, tn), lambda i, j: (l, 0, j)),
                  pl.BlockSpec((None, D, tn), lambda i, j: (l, 0, j + nj)),
                  pl.BlockSpec((3, tn), lambda i, j: (0, j)),
                  pl.BlockSpec((3, tn), lambda i, j: (0, j + nj)),
                  pl.BlockSpec((1, tn), lambda i, j: (0, j)),
                  pl.BlockSpec((1, tn), lambda i, j: (0, j + nj)),
                  pl.BlockSpec((None, tn, D), lambda i, j: (l, j, 0)),
                  pl.BlockSpec((1, D), lambda i, j: (0, 0))],
        out_specs=pl.BlockSpec((tm, D), lambda i, j: (i, 0), pipeline_mode=pl.Buffered(1)),
        out_shape=jax.ShapeDtypeStruct((n_rows, D), F32),
        scratch_shapes=[pltpu.VMEM((tm, D), BF16)],
        compiler_params=_params(("parallel", "arbitrary"),
                                3 * tm * D * 4 + tm * D * 2 + 6 * D * tn * 2 + 16 * tm * FFN_SUB * 4),
        name="conv_ffn",
    )(x, g_pre.reshape(1, D), modg, w_up_bf16, w_up_bf16, conv_w, conv_w, cb, cb, w_down_bf16,
      g_post.reshape(1, D))


def _layout_w_in(w):
    o = np.cumsum([0, Q_LORA, KV_LORA, MLA_ROPE, FNET_WIDTH, 3 * RWKV_WIDTH, W_LORA, A_LORA, G_LORA])
    q_dn, kv_dn, k_rope, xf, rkv, w_lo, a_lo, g_lo = [w[..., o[i]:o[i + 1]] for i in range(8)]
    zeros = jnp.zeros(w.shape[:-1] + (LANE - MLA_ROPE,), w.dtype)
    tail = jnp.zeros(w.shape[:-1] + (IN_PAD - COL_GLO - G_LORA,), w.dtype)
    return jnp.concatenate([rkv, q_dn, xf, kv_dn, k_rope, zeros, w_lo, a_lo, g_lo, tail], axis=-1).astype(BF16)


def _layout_w_uq(w):
    L = w.shape[0]
    w = w.reshape(L, Q_LORA, MLA_HEADS, MLA_NOPE + MLA_ROPE)
    w = jnp.pad(w, ((0, 0), (0, 0), (0, 0), (0, HEAD_PAD - MLA_NOPE - MLA_ROPE)))
    return w.reshape(L, Q_LORA, MLA_HEADS * HEAD_PAD).astype(BF16)


def _layout_w_ukv(w):
    L = w.shape[0]
    w = w.reshape(L, KV_LORA, MLA_HEADS, MLA_NOPE + MLA_V)
    wk = w[..., :MLA_NOPE].reshape(L, KV_LORA, MLA_HEADS * MLA_NOPE)
    wv = w[..., MLA_NOPE:].reshape(L, KV_LORA, MLA_HEADS * MLA_V)
    return wk.astype(BF16), wv.astype(BF16)


def _rope_table(tm):
    t = jnp.arange(DEC_SEQ)
    rowp = (t // GRID_W).astype(F32)
    colp = (t % GRID_W).astype(F32)
    n = MLA_ROPE // 4
    inv = ROPE_BASE ** (-jnp.arange(n, dtype=F32) / n)
    ang = jnp.concatenate([rowp[:, None] * inv, colp[:, None] * inv], axis=-1)
    cos = jnp.concatenate([jnp.ones((tm, MLA_ROPE // 2), F32), jnp.cos(ang)], axis=0)
    sin = jnp.concatenate([jnp.zeros((tm, MLA_ROPE // 2), F32), jnp.sin(ang)], axis=0)
    z32 = jnp.zeros_like(cos)
    z64 = jnp.zeros((cos.shape[0], LANE - MLA_ROPE), F32)
    mul = jnp.concatenate([cos, cos, z64], axis=1)
    left = jnp.concatenate([-sin, z32, z64], axis=1)
    right = jnp.concatenate([z32, sin, z64], axis=1)
    return jnp.stack([mul, left, right])


def kernel(x_prompt, x_sample, cache_mla_ckv, cache_mla_krope, state_rwkv, c, c_ctx, w_mod, b_mod, g_pre_mix, g_post_mix, g_pre_ffn, g_post_ffn, w_in, g_q_norm, w_uq, g_kv_norm, w_ukv, rwkv_conv, rwkv_w0, rwkv_w2, rwkv_a0, rwkv_a2, rwkv_g2, rwkv_k_k, rwkv_k_a, rwkv_r_k, rwkv_gn_g, rwkv_gn_b, w_out, ffn_w_up, ffn_conv, ffn_conv_b, ffn_w_down):
    Bc, Tc, D = x_prompt.shape
    Bl, Tl, _ = x_sample.shape
    L = w_mod.shape[0]
    assert (Tc, Tl, D) == (SEQ, DEC_SEQ, D_MODEL) and cache_mla_ckv.shape[2] == PAST_LEN
    assert Bl + 1 <= 8 and (Bc * Tc) % DEC_SEQ == 0
    n_ctx = Bc * Tc
    n_lat = Bl * Tl
    x = jnp.concatenate([x_prompt.reshape(n_ctx, D), x_sample.reshape(n_lat, D)], axis=0)

    cond = jnp.zeros((8, D), F32).at[0].set(c_ctx).at[1:1 + Bl].set(c)
    mods = _modulation(cond, w_mod, b_mod)
    group_row = np.concatenate([np.zeros(n_ctx // ROW_GROUP, np.int32),
                                1 + np.arange(n_lat // ROW_GROUP, dtype=np.int32) // (Tl // ROW_GROUP)])

    tm_mla = 512
    rope_tab = _rope_table(tm_mla)
    ridx = _rope_index(tm_mla, n_ctx)
    ridx_cache = lambda i: 0
    zero_state = jnp.zeros((Bc, 2, RWKV_HEADS, RWKV_HEAD_DIM, RWKV_HEAD_DIM), F32)

    w_in_b = _layout_w_in(w_in)
    w_uq_b = _layout_w_uq(w_uq)
    wk_b, wv_b = _layout_w_ukv(w_ukv)
    w_out_b = w_out.astype(BF16)
    w_up_b = ffn_w_up.astype(BF16)
    w_down_b = ffn_w_down.astype(BF16)

    ckv_out, krope_out, state_out = [], [], []
    for l in range(L):
        modg = mods[l][group_row].reshape(-1, 6, D)
        p = dict(rwkv_conv=rwkv_conv[l], rwkv_w0=rwkv_w0[l], rwkv_w2=rwkv_w2[l], rwkv_a0=rwkv_a0[l],
                 rwkv_a2=rwkv_a2[l], rwkv_g2=rwkv_g2[l], rwkv_k_k=rwkv_k_k[l], rwkv_k_a=rwkv_k_a[l],
                 rwkv_r_k=rwkv_r_k[l])

        hin = _in_projection(x, g_pre_mix[l], modg, w_in_b, l)

        q = _q_projection(hin, g_q_norm[l], w_uq_b, l, rope_tab, n_ctx)
        ckv, k, v = _kv_projection(hin, COL_KVDN // KV_LORA, COL_KROPE // LANE, g_kv_norm[l], wk_b, wv_b, l,
                                   rope_tab, ridx, tm_mla, True)
        cache_src = jnp.concatenate(
            [cache_mla_ckv[:, l].reshape(Bl * PAST_LEN, KV_LORA),
             jnp.pad(cache_mla_krope[:, l].reshape(Bl * PAST_LEN, MLA_ROPE), ((0, 0), (0, LANE - MLA_ROPE)))],
            axis=1)
        k_c, v_c = _kv_projection(cache_src, 0, KV_LORA // LANE, g_kv_norm[l], wk_b, wv_b, l,
                                  rope_tab, ridx_cache, PAST_LEN, False)
        attn = (_attention(q, 0, Bc, Tc, [(k, v, 0, Tc)]),
                _attention(q, n_ctx, Bl, Tl, [(k_c, v_c, 0, PAST_LEN), (k, v, n_ctx, Tl)]))

        four = (_fourier_mix(hin, 0, Bc, Tc), _fourier_mix(hin, n_ctx, Bl, Tl))

        prep = _rwkv_prep(hin, p, n_ctx)
        yf_ctx, yb_ctx, s_ctx = _rwkv_scan(prep, 0, Bc, Tc, zero_state)
        yf_lat, yb_lat, _ = _rwkv_scan(prep, n_ctx, Bl, Tl, state_rwkv[:, l])
        rw = _rwkv_finalize(list(yf_ctx) + list(yf_lat), list(yb_ctx) + list(yb_lat), prep[4], prep[3],
                            rwkv_gn_g[l], rwkv_gn_b[l])

        x = _out_projection(attn, four, rw, w_out_b, l, x, g_post_mix[l], modg)
        ffn = functools.partial(_conv_ffn, x, g_pre_ffn[l], g_post_ffn[l], modg, w_up_b, w_down_b, l,
                                ffn_conv[l], ffn_conv_b[l], n_ctx)
        if l + 1 < L:
            x = ffn(0, n_ctx + n_lat)
        else:
            y_prompt, y_sample = ffn(0, n_ctx), ffn(n_ctx, n_lat)

        ckv_out.append(ckv[:n_ctx].reshape(Bc, Tc, KV_LORA))
        krope_out.append(hin[:n_ctx, COL_KROPE:COL_KROPE + MLA_ROPE].reshape(Bc, Tc, MLA_ROPE))
        state_out.append(s_ctx)

    return (y_prompt.reshape(Bc, Tc, D), y_sample.reshape(Bl, Tl, D),
            jnp.stack(ckv_out, axis=1), jnp.stack(krope_out, axis=1), jnp.stack(state_out, axis=1))
```

```python
import functools
import math

import jax
import jax.numpy as jnp
import numpy as np
from jax import lax
from jax.experimental import pallas as pl
from jax.experimental.pallas import tpu as pltpu

F32 = jnp.float32
BF16 = jnp.bfloat16

D_MODEL = 2048
SEQ = 256
DEC_SEQ = 1024
PAST_LEN = 256
GRID_W = 64
MLA_HEADS = 8
MLA_NOPE = 128
MLA_ROPE = 64
MLA_V = 128
Q_LORA = 512
KV_LORA = 256
FNET_GROUPS = 4
FNET_GROUP_DIM = 128
FNET_WIDTH = FNET_GROUPS * FNET_GROUP_DIM
RWKV_HEADS = 8
RWKV_HEAD_DIM = 64
RWKV_WIDTH = RWKV_HEADS * RWKV_HEAD_DIM
W_LORA = 64
A_LORA = 64
G_LORA = 128
D_FF = 5632
ROPE_BASE = 10000.0
EPS = 1e-6
GN_EPS = 64e-5
DECAY_SCALE = math.exp(-0.5)

LANE = 128
ROW_GROUP = 256
HEAD_PAD = 256
CHUNK = 64
VMEM_CAP = 60 * 1024 * 1024

COL_RKV = 0
COL_QDN = 3 * RWKV_WIDTH
COL_XF = COL_QDN + Q_LORA
COL_KVDN = COL_XF + FNET_WIDTH
COL_KROPE = COL_KVDN + KV_LORA
COL_WA = COL_KROPE + LANE
COL_GLO = COL_WA + LANE
MXU_COLS = 256
IN_PAD = -(-(COL_GLO + G_LORA) // MXU_COLS) * MXU_COLS


SMALL_OPERANDS = 8 * 1024 * 1024


def _params(sem, nbytes):
    return pltpu.CompilerParams(dimension_semantics=sem,
                                vmem_limit_bytes=int(min(VMEM_CAP, nbytes + SMALL_OPERANDS)))


def _dot(a, b):
    return jnp.dot(a, b, preferred_element_type=F32)


def _dot_nt(a, b):
    return lax.dot_general(a, b, (((1,), (1,)), ((), ())), preferred_element_type=F32)


def _dot_tn(a, b):
    return lax.dot_general(a, b, (((0,), (0,)), ((), ())), preferred_element_type=F32)


def _split_bf16(x, n):
    terms = []
    rem = x
    for i in range(n):
        t = rem.astype(BF16)
        terms.append(t)
        if i + 1 < n:
            rem = rem - t.astype(F32)
    return terms


def _dot_exact_rhs(a, b_exact_bf16, n):
    acc = None
    for t in _split_bf16(a, n):
        p = _dot(t, b_exact_bf16)
        acc = p if acc is None else acc + p
    return acc


def _rms(x, g):
    return x * lax.rsqrt(jnp.mean(x * x, axis=-1, keepdims=True) + EPS) * g


def _sigmoid(x):
    return 1.0 / (1.0 + jnp.exp(-x))


def _mod_kernel(cond_ref, w_ref, b_ref, o_ref):
    c = cond_ref[...]
    s = (c * _sigmoid(c)).astype(BF16)
    o_ref[0] = _dot(s, w_ref[0].astype(BF16)) + b_ref[0]


def _modulation(cond, w_mod, b_mod):
    L, D, N = w_mod.shape
    tn = 1024
    return pl.pallas_call(
        _mod_kernel,
        grid=(L, N // tn),
        in_specs=[pl.BlockSpec((8, D), lambda l, j: (0, 0)),
                  pl.BlockSpec((1, D, tn), lambda l, j: (l, 0, j)),
                  pl.BlockSpec((1, 1, tn), lambda l, j: (l, 0, j))],
        out_specs=pl.BlockSpec((1, 8, tn), lambda l, j: (l, 0, j)),
        out_shape=jax.ShapeDtypeStruct((L, 8, N), F32),
        compiler_params=_params(("parallel", "parallel"), 3 * D * tn * 4),
        name="modulation",
    )(cond, w_mod, b_mod.reshape(L, 1, N))


def _inproj_kernel(x_ref, g_ref, mod_ref, w_ref, o_ref):
    h = _rms(x_ref[...], g_ref[...]) * (1.0 + mod_ref[0, 1:2, :]) + mod_ref[0, 0:1, :]
    o_ref[...] = _dot(h.astype(BF16), w_ref[...])


def _in_projection(x, g, modg, w_bf16, l):
    R, D = x.shape
    N = w_bf16.shape[2]
    tm = 512
    return pl.pallas_call(
        _inproj_kernel,
        grid=(R // tm,),
        in_specs=[pl.BlockSpec((tm, D), lambda i: (i, 0)),
                  pl.BlockSpec((1, D), lambda i: (0, 0)),
                  pl.BlockSpec((1, 6, D), lambda i: (i * (tm // ROW_GROUP), 0, 0)),
                  pl.BlockSpec((None, D, N), lambda i: (l, 0, 0), pipeline_mode=pl.Buffered(1))],
        out_specs=pl.BlockSpec((tm, N), lambda i: (i, 0)),
        out_shape=jax.ShapeDtypeStruct((R, N), F32),
        compiler_params=_params(("parallel",), 2 * tm * D * 4 + tm * D * 2 + D * N * 2 + 3 * tm * N * 4),
        name="in_projection",
    )(x, g.reshape(1, D), modg, w_bf16)


def _rope_block(blk, rope_ref):
    return (blk * rope_ref[0]
            + pltpu.roll(blk, LANE - MLA_ROPE // 2, 1) * rope_ref[1]
            + pltpu.roll(blk, MLA_ROPE // 2, 1) * rope_ref[2])


def _q_kernel(qdn_ref, g_ref, w_ref, rope_ref, o_ref):
    scale = (MLA_NOPE + MLA_ROPE) ** -0.5
    qn = _rms(qdn_ref[...], g_ref[...]).astype(BF16)
    q = _dot(qn, w_ref[...]) * scale
    parts = []
    for h in range(MLA_HEADS):
        base = h * HEAD_PAD
        parts.append(q[:, base:base + MLA_NOPE])
        parts.append(_rope_block(q[:, base + MLA_NOPE:base + HEAD_PAD], rope_ref))
    o_ref[...] = jnp.concatenate(parts, axis=1).astype(o_ref.dtype)


def _rope_index(tm, n_ctx_rows):
    n_ctx_tiles = n_ctx_rows // tm
    per_seq = DEC_SEQ // tm

    def index(i):
        return jnp.where(i < n_ctx_tiles, 0, 1 + (i - n_ctx_tiles) % per_seq)
    return index


def _q_projection(hin, g, w_bf16, l, rope_tab, n_ctx_rows):
    R = hin.shape[0]
    tm = 512
    N = MLA_HEADS * HEAD_PAD
    ridx = _rope_index(tm, n_ctx_rows)
    return pl.pallas_call(
        _q_kernel,
        grid=(R // tm,),
        in_specs=[pl.BlockSpec((tm, Q_LORA), lambda i: (i, COL_QDN // Q_LORA)),
                  pl.BlockSpec((1, Q_LORA), lambda i: (0, 0)),
                  pl.BlockSpec((None, Q_LORA, N), lambda i: (l, 0, 0)),
                  pl.BlockSpec((3, tm, LANE), lambda i: (0, ridx(i), 0))],
        out_specs=pl.BlockSpec((tm, N), lambda i: (i, 0)),
        out_shape=jax.ShapeDtypeStruct((R, N), BF16),
        compiler_params=_params(("parallel",), 2 * Q_LORA * N * 2 + 6 * tm * N * 4),
        name="q_projection",
    )(hin, g.reshape(1, Q_LORA), w_bf16, rope_tab)


def _kv_kernel(kvdn_ref, kr_ref, g_ref, wk_ref, wv_ref, rope_ref, *out_refs, normalize):
    k_ref, v_ref = out_refs[-2:]
    c = kvdn_ref[...]
    if normalize:
        c = _rms(c, g_ref[...])
        out_refs[0][...] = c
    cb = c.astype(BF16)
    kn = _dot(cb, wk_ref[...])
    v_ref[...] = _dot(cb, wv_ref[...]).astype(v_ref.dtype)
    kr = _rope_block(kr_ref[...], rope_ref)
    parts = []
    for h in range(MLA_HEADS):
        parts.append(kn[:, h * MLA_NOPE:(h + 1) * MLA_NOPE])
        parts.append(kr)
    k_ref[...] = jnp.concatenate(parts, axis=1).astype(k_ref.dtype)


def _kv_projection(src, kv_col, kr_col, g, wk_bf16, wv_bf16, l, rope_tab, ridx, tm, normalize):
    R = src.shape[0]
    NK = MLA_HEADS * HEAD_PAD
    NV = MLA_HEADS * MLA_V
    return pl.pallas_call(
        functools.partial(_kv_kernel, normalize=normalize),
        grid=(R // tm,),
        in_specs=[pl.BlockSpec((tm, KV_LORA), lambda i: (i, kv_col)),
                  pl.BlockSpec((tm, LANE), lambda i: (i, kr_col)),
                  pl.BlockSpec((1, KV_LORA), lambda i: (0, 0)),
                  pl.BlockSpec((None, KV_LORA, MLA_HEADS * MLA_NOPE), lambda i: (l, 0, 0)),
                  pl.BlockSpec((None, KV_LORA, NV), lambda i: (l, 0, 0)),
                  pl.BlockSpec((3, tm, LANE), lambda i: (0, ridx(i), 0))],
        out_specs=([pl.BlockSpec((tm, KV_LORA), lambda i: (i, 0))] if normalize else [])
                  + [pl.BlockSpec((tm, NK), lambda i: (i, 0)), pl.BlockSpec((tm, NV), lambda i: (i, 0))],
        out_shape=([jax.ShapeDtypeStruct((R, KV_LORA), F32)] if normalize else [])
                  + [jax.ShapeDtypeStruct((R, NK), BF16), jax.ShapeDtypeStruct((R, NV), BF16)],
        compiler_params=_params(("parallel",), 8 * tm * NK * 4),
        name="kv_projection" if normalize else "kv_projection_cache",
    )(src, src, g.reshape(1, KV_LORA), wk_bf16, wv_bf16, rope_tab)


def _attn_kernel(q_ref, *refs):
    o_ref = refs[-1]
    kv = [(refs[i], refs[i + 1]) for i in range(0, len(refs) - 1, 2)]
    for h in range(MLA_HEADS):
        qk = slice(h * HEAD_PAD, (h + 1) * HEAD_PAD)
        vo = slice(h * MLA_V, (h + 1) * MLA_V)
        q = q_ref[:, qk]
        scores = [_dot_nt(q, k_ref[:, qk]) for k_ref, _ in kv]
        m = functools.reduce(jnp.maximum, [jnp.max(s, axis=-1, keepdims=True) for s in scores])
        probs = [jnp.exp(s - m) for s in scores]
        l = functools.reduce(jnp.add, [jnp.sum(p, axis=-1, keepdims=True) for p in probs])
        o = functools.reduce(jnp.add, [_dot(p.astype(BF16), v_ref[:, vo]) for p, (_, v_ref) in zip(probs, kv)])
        o_ref[:, vo] = (o / l).astype(o_ref.dtype)


def _attention(q, q_row0, n_batch, tq_total, sources):
    tq = min(512, tq_total)
    nq = tq_total // tq
    q0 = q_row0 // tq
    NQ = MLA_HEADS * HEAD_PAD
    NV = MLA_HEADS * MLA_V
    in_specs = [pl.BlockSpec((tq, NQ), lambda b, i: (q0 + b * nq + i, 0))]
    args = [q]
    tk_total = 0
    for k, v, row0, tk in sources:
        in_specs.append(pl.BlockSpec((tk, NQ), lambda b, i, o=row0 // tk: (o + b, 0)))
        in_specs.append(pl.BlockSpec((tk, NV), lambda b, i, o=row0 // tk: (o + b, 0)))
        args += [k, v]
        tk_total += tk
    return pl.pallas_call(
        _attn_kernel,
        grid=(n_batch, nq),
        in_specs=in_specs,
        out_specs=pl.BlockSpec((tq, NV), lambda b, i: (b * nq + i, 0)),
        out_shape=jax.ShapeDtypeStruct((n_batch * tq_total, NV), BF16),
        compiler_params=_params(("parallel", "arbitrary"),
                                2 * (tq + tk_total) * (NQ + NV) * 2 + 6 * tq * tk_total * 4),
        name="attention",
    )(*args)


def _fnet_kernel(z_ref, cs_ref, ct_ref, o_ref):
    yc, ys = [], []
    for g in range(FNET_GROUPS):
        zg = z_ref[:, g * FNET_GROUP_DIM:(g + 1) * FNET_GROUP_DIM].astype(BF16)
        y = _dot(zg, cs_ref[...])
        yc.append(y[:, :FNET_GROUP_DIM])
        ys.append(y[:, FNET_GROUP_DIM:])
    stacked = jnp.concatenate([jnp.concatenate(yc, axis=1), jnp.concatenate(ys, axis=1)], axis=0)
    o_ref[...] = _dot(ct_ref[...], stacked.astype(BF16)).astype(o_ref.dtype)


def _dft_tables(T):
    def cs(n):
        i = jnp.arange(n, dtype=jnp.int32)
        ang = ((i[:, None] * i[None, :]) % n).astype(F32) * (2.0 * math.pi / n)
        return jnp.cos(ang), jnp.sin(ang)
    cc, sc = cs(FNET_GROUP_DIM)
    ct, st = cs(T)
    norm = 1.0 / math.sqrt(T * FNET_GROUP_DIM)
    return (jnp.concatenate([cc, sc], axis=1).astype(BF16),
            (jnp.concatenate([ct, -st], axis=1) * norm).astype(BF16))


def _fourier_mix(hin, row0, n_batch, T):
    cs_tab, ct_tab = _dft_tables(T)
    b0 = row0 // T
    return pl.pallas_call(
        _fnet_kernel,
        grid=(n_batch,),
        in_specs=[pl.BlockSpec((T, FNET_WIDTH), lambda b: (b0 + b, COL_XF // FNET_WIDTH)),
                  pl.BlockSpec((FNET_GROUP_DIM, 2 * FNET_GROUP_DIM), lambda b: (0, 0)),
                  pl.BlockSpec((T, 2 * T), lambda b: (0, 0))],
        out_specs=pl.BlockSpec((T, FNET_WIDTH), lambda b: (b, 0)),
        out_shape=jax.ShapeDtypeStruct((n_batch * T, FNET_WIDTH), BF16),
        compiler_params=_params(("parallel",), 2 * T * 2 * T * 2 + 10 * T * FNET_WIDTH * 4),
        name="fourier_mix",
    )(hin, cs_tab, ct_tab)


def _head_sum_matrix(scale):
    i = np.arange(RWKV_WIDTH) // RWKV_HEAD_DIM
    return jnp.asarray((i[:, None] == i[None, :]).astype(np.float32) * scale, BF16)


def _rwkv_prep_kernel(x_ref, xp_ref, xn_ref, wa_ref, gl_ref, cw_ref, w0_ref, w2_ref, a0_ref, a2_ref, g2_ref,
                      kk_ref, ka_ref, rk_ref, hs_ref,
                      r_ref, kh_ref, v_ref, g_ref, bonus_ref, lw_ref, kt_ref, beta_ref,
                      *, n_ctx_tiles, tiles_per_seq):
    i = pl.program_id(0)
    j = i - n_ctx_tiles
    first = jnp.logical_or(i < n_ctx_tiles, j % tiles_per_seq == 0)
    last = jnp.logical_or(i < n_ctx_tiles, j % tiles_per_seq == tiles_per_seq - 1)
    x = x_ref[...]
    tm = x.shape[0]
    row = lax.broadcasted_iota(jnp.int32, x.shape, 0)
    prev_row = jnp.where(first, 0.0, xp_ref[7:8, :])
    next_row = jnp.where(last, 0.0, xn_ref[0:1, :])
    x_prev = jnp.where(row == 0, prev_row, pltpu.roll(x, 1, 0))
    x_next = jnp.where(row == tm - 1, next_row, pltpu.roll(x, tm - 1, 0))
    xc = x_prev * cw_ref[0:1, :] + x * cw_ref[1:2, :] + x_next * cw_ref[2:3, :]
    R_ = RWKV_WIDTH
    r, k, v = xc[:, :R_], xc[:, R_:2 * R_], xc[:, 2 * R_:]
    wl = jnp.tanh(wa_ref[:, :W_LORA]).astype(BF16)
    al = wa_ref[:, W_LORA:W_LORA + A_LORA].astype(BF16)
    g_ref[...] = _dot(_sigmoid(gl_ref[...]).astype(BF16), g2_ref[...].astype(BF16))
    kappa = k * kk_ref[...]
    ss = _dot_exact_rhs(kappa * kappa, hs_ref[...], 3)
    kh = kappa * lax.rsqrt(ss + EPS)
    kt_sum = None
    for d in range(2):
        lw_ref[d] = -DECAY_SCALE * _sigmoid(w0_ref[d:d + 1, :] + _dot(wl, w2_ref[d].astype(BF16)))
        a = _sigmoid(a0_ref[d:d + 1, :] + _dot(al, a2_ref[d].astype(BF16)))
        kt = k * (1.0 + (a - 1.0) * ka_ref[...])
        kt_ref[d] = kt.astype(kt_ref.dtype)
        beta_ref[d] = (a * kh).astype(beta_ref.dtype)
        kt_sum = kt if kt_sum is None else kt_sum + kt
    r_ref[...] = r.astype(r_ref.dtype)
    kh_ref[...] = kh.astype(kh_ref.dtype)
    v_ref[...] = v.astype(v_ref.dtype)
    bonus_ref[...] = _dot_exact_rhs(r * kt_sum * rk_ref[...], hs_ref[...], 3) * v


def _rwkv_prep(hin, p, n_ctx_rows):
    R = hin.shape[0]
    tm = ROW_GROUP
    W3 = 3 * RWKV_WIDTH
    nh = tm // 8
    last_blk = R // 8 - 1
    row = lambda a: a.reshape(1, RWKV_WIDTH)
    full = lambda *s: pl.BlockSpec(s, lambda i: (0,) * len(s))
    out_rw = lambda dt: jax.ShapeDtypeStruct((R, RWKV_WIDTH), dt)
    out_rw2 = lambda dt: jax.ShapeDtypeStruct((2, R, RWKV_WIDTH), dt)
    spec_rw = pl.BlockSpec((tm, RWKV_WIDTH), lambda i: (i, 0))
    spec_rw2 = pl.BlockSpec((2, tm, RWKV_WIDTH), lambda i: (0, i, 0))
    return pl.pallas_call(
        functools.partial(_rwkv_prep_kernel, n_ctx_tiles=n_ctx_rows // tm, tiles_per_seq=DEC_SEQ // tm),
        grid=(R // tm,),
        in_specs=[pl.BlockSpec((tm, W3), lambda i: (i, COL_RKV // W3)),
                  pl.BlockSpec((8, W3), lambda i: (jnp.maximum(i * nh - 1, 0), 0)),
                  pl.BlockSpec((8, W3), lambda i: (jnp.minimum((i + 1) * nh, last_blk), 0)),
                  pl.BlockSpec((tm, LANE), lambda i: (i, COL_WA // LANE)),
                  pl.BlockSpec((tm, G_LORA), lambda i: (i, COL_GLO // G_LORA)),
                  full(3, W3), full(2, RWKV_WIDTH), full(2, W_LORA, RWKV_WIDTH),
                  full(2, RWKV_WIDTH), full(2, A_LORA, RWKV_WIDTH), full(G_LORA, RWKV_WIDTH),
                  full(1, RWKV_WIDTH), full(1, RWKV_WIDTH), full(1, RWKV_WIDTH),
                  full(RWKV_WIDTH, RWKV_WIDTH)],
        out_specs=[spec_rw] * 5 + [spec_rw2] * 3,
        out_shape=[out_rw(BF16)] * 3 + [out_rw(F32)] * 2 + [out_rw2(F32), out_rw2(BF16), out_rw2(BF16)],
        compiler_params=_params(("parallel",), 40 * tm * W3 * 4),
        name="rwkv_prep",
    )(hin, hin, hin, hin, hin, p['rwkv_conv'], p['rwkv_w0'], p['rwkv_w2'], p['rwkv_a0'], p['rwkv_a2'],
      p['rwkv_g2'], row(p['rwkv_k_k']), row(p['rwkv_k_a']), row(p['rwkv_r_k']), _head_sum_matrix(1.0))


def _rwkv_scan_kernel(*refs, n_seq):
    seq_refs = [refs[13 * p:13 * (p + 1)] for p in range(n_seq)]
    mi_ref = refs[13 * n_seq]
    out_refs = refs[13 * n_seq + 1:-1]
    s_scr = refs[-1]
    c = pl.program_id(1)
    nc = pl.num_programs(1)
    N = RWKV_HEAD_DIM
    C = CHUNK
    H = RWKV_HEADS

    @pl.when(c == 0)
    def _():
        for p in range(n_seq):
            s_scr[p] = seq_refs[p][12][0]

    eye = (lax.broadcasted_iota(jnp.int32, (C, C), 0) == lax.broadcasted_iota(jnp.int32, (C, C), 1)).astype(F32)
    dirs = [(p, d, tuple(sr[3 * d:3 * d + 3]) + tuple(sr[6 + 3 * d:9 + 3 * d]))
            for p, sr in enumerate(seq_refs) for d in range(2)]
    chains = []
    for p, d, (r_ref, kh_ref, v_ref, lw_ref, kt_ref, beta_ref) in dirs:
        mi = mi_ref[d]
        lw = lw_ref[0]
        cl = _dot_exact_lhs(mi.astype(BF16), lw, 3)
        cl_tot = jnp.sum(lw, axis=0, keepdims=True)
        g_inv = jnp.exp(-cl)
        g_end = jnp.exp(cl_tot - cl)
        g_tot = jnp.exp(cl_tot)
        kh, beta, kt = kh_ref[...], beta_ref[0], kt_ref[0]
        k_dec = kh * jnp.exp(cl - lw)
        r_dec = r_ref[...] * jnp.exp(cl)
        b_inv, k_inv = beta * g_inv, kt * g_inv
        k_end, b_end = kt * g_end, beta * g_end
        v_all = v_ref[...]
        for h in range(H):
            sl = slice(h * N, (h + 1) * N)
            chains.append(dict(p=p, d=d, h=h, mi=mi, ms=mi - eye, kd=k_dec[:, sl], rd=r_dec[:, sl],
                               bi=b_inv[:, sl], ki=k_inv[:, sl], ke=k_end[:, sl], be=b_end[:, sl],
                               v=v_all[:, sl], gt=g_tot[:, sl], s0=s_scr[p, d, h]))

    for ch in chains:
        lhs = jnp.concatenate([ch['kd'], ch['rd']], axis=0).astype(BF16)
        rhs = jnp.concatenate([ch['bi'], ch['ki']], axis=0).astype(BF16)
        nt = _dot_nt(lhs, rhs)
        ch['a'] = nt[:C, :C] * ch['ms']
        ch['b'] = (nt[:C, C:] * ch['ms']).astype(BF16)
        ch['qp'] = (nt[C:, :] * jnp.concatenate([-ch['mi'], ch['mi']], axis=1)).astype(BF16)
        ch['lhs'] = lhs
        ch['t'] = eye - ch['a']
    for ch in chains:
        xb = (-ch['a']).astype(BF16)
        ch['xp'] = _dot(xb, xb).astype(BF16)
    n_sq = int(math.log2(C)) - 1
    for k in range(n_sq):
        last = k == n_sq - 1
        for ch in chains:
            tb = ch['t'].astype(BF16)
            prod = _dot(tb if last else jnp.concatenate([tb, ch['xp']], axis=0), ch['xp'])
            ch['t'] = ch['t'] + prod[:C]
            if not last:
                ch['xp'] = prod[C:].astype(BF16)
    for ch in chains:
        ch['s0b'] = ch['s0'].astype(BF16)
        ch['vb'] = ch['v'].astype(BF16)
        ks = _dot_nt(ch['lhs'], ch['s0b'])
        ch['rs'] = ks[C:]
        ch['z'] = ks[:C] + _dot(ch['b'], ch['vb'])
    for ch in chains:
        u = _dot(ch['t'].astype(BF16), ch['z'].astype(BF16))
        ch['uv'] = jnp.concatenate([u.astype(BF16), ch['vb']], axis=0)
    for ch in chains:
        kb = jnp.concatenate([-ch['be'], ch['ke']], axis=0).astype(BF16)
        s_scr[ch['p'], ch['d'], ch['h']] = ch['s0'] * ch['gt'] + _dot_tn(ch['uv'], kb)
    for ch in chains:
        ch['y'] = ch['rs'] + _dot(ch['qp'], ch['uv'])
    for p in range(n_seq):
        for d in range(2):
            ys = [ch['y'] for ch in chains[(2 * p + d) * H:(2 * p + d + 1) * H]]
            out_refs[3 * p + d][...] = jnp.concatenate(ys, axis=1)

    @pl.when(c == nc - 1)
    def _():
        for p in range(n_seq):
            out_refs[3 * p + 2][0] = s_scr[p]


def _dot_exact_lhs(a_exact_bf16, b, n):
    acc = None
    for t in _split_bf16(b, n):
        p = _dot(a_exact_bf16, t)
        acc = p if acc is None else acc + p
    return acc


def _causal_masks():
    t = np.arange(CHUNK)
    fwd = (t[None, :] <= t[:, None]).astype(np.float32)
    return jnp.asarray(np.stack([fwd, fwd.T]))


def _rwkv_scan(prep, row0, n_batch, T, s0):
    r, kh, v, _, _, lw, kt, beta = prep
    C = CHUNK
    nc = T // C
    c0 = row0 // C
    W = RWKV_WIDTH
    n_seq = 2 if n_batch % 2 == 0 else 1
    nb = n_batch // n_seq
    st_shape = (1, 2, RWKV_HEADS, RWKV_HEAD_DIM, RWKV_HEAD_DIM)
    in_specs, args, out_specs, out_shape = [], [], [], []
    for p in range(n_seq):
        fwd = lambda b, c, p=p: (p * nb + b) * nc + c
        bwd = lambda b, c, p=p: (p * nb + b) * nc + nc - 1 - c
        shared_f = pl.BlockSpec((C, W), lambda b, c, f=fwd: (c0 + f(b, c), 0))
        shared_b = pl.BlockSpec((C, W), lambda b, c, f=bwd: (c0 + f(b, c), 0))
        dir_f = pl.BlockSpec((1, C, W), lambda b, c, f=fwd: (0, c0 + f(b, c), 0))
        dir_b = pl.BlockSpec((1, C, W), lambda b, c, f=bwd: (1, c0 + f(b, c), 0))
        in_specs += [shared_f] * 3 + [shared_b] * 3 + [dir_f] * 3 + [dir_b] * 3
        in_specs.append(pl.BlockSpec(st_shape, lambda b, c, p=p: (p * nb + b, 0, 0, 0, 0)))
        args += [r, kh, v, r, kh, v, lw, kt, beta, lw, kt, beta, s0]
        out_specs += [pl.BlockSpec((C, W), lambda b, c: (b * nc + c, 0)),
                      pl.BlockSpec((C, W), lambda b, c: (b * nc + nc - 1 - c, 0)),
                      pl.BlockSpec(st_shape, lambda b, c: (b, 0, 0, 0, 0))]
        out_shape += [jax.ShapeDtypeStruct((nb * T, W), F32)] * 2
        out_shape.append(jax.ShapeDtypeStruct((nb,) + st_shape[1:], F32))
    in_specs.append(pl.BlockSpec((2, C, C), lambda b, c: (0, 0, 0)))
    outs = pl.pallas_call(
        functools.partial(_rwkv_scan_kernel, n_seq=n_seq),
        grid=(nb, nc),
        in_specs=in_specs,
        out_specs=out_specs,
        out_shape=out_shape,
        scratch_shapes=[pltpu.VMEM((n_seq, 2, RWKV_HEADS, RWKV_HEAD_DIM, RWKV_HEAD_DIM), F32)],
        compiler_params=_params(("parallel", "arbitrary"), 32 * 1024 * 1024),
        name="rwkv_scan",
    )(*args, _causal_masks())
    return outs[0::3], outs[1::3], jnp.concatenate(outs[2::3], axis=0)


def _part_tiles(parts, tm):
    return [int(t) for t in np.cumsum([0] + [p.shape[0] // tm for p in parts])]


def _part_specs(parts, tm):
    starts = _part_tiles(parts, tm)
    return [pl.BlockSpec((tm, p.shape[1]), lambda i, t0=t0, n=t1 - t0: (jnp.clip(i - t0, 0, n - 1), 0))
            for p, t0, t1 in zip(parts, starts[:-1], starts[1:])]


def _for_own_part(starts, run):
    i = pl.program_id(0)
    for k, (t0, t1) in enumerate(zip(starts[:-1], starts[1:])):
        pl.when(jnp.logical_and(i >= t0, i < t1))(functools.partial(run, k))


def _rwkv_fin_kernel(*refs, starts):
    n = len(starts) - 1
    yf_refs, yb_refs = refs[:n], refs[n:2 * n]
    bonus_ref, g_ref, gg_ref, gb_ref, hm_ref, o_ref = refs[2 * n:]

    def run(k):
        y = yf_refs[k][...] + yb_refs[k][...]
        mu = _dot_exact_rhs(y, hm_ref[...], 3)
        yc = y - mu
        var = _dot_exact_rhs(yc * yc, hm_ref[...], 3)
        yn = yc * lax.rsqrt(var + GN_EPS) * gg_ref[...] + gb_ref[...]
        o_ref[...] = ((yn + bonus_ref[...]) * g_ref[...]).astype(o_ref.dtype)

    _for_own_part(starts, run)


def _rwkv_finalize(yf_parts, yb_parts, bonus, g, gn_g, gn_b):
    tm = 512
    W = RWKV_WIDTH
    starts = _part_tiles(yf_parts, tm)
    rows = pl.BlockSpec((tm, W), lambda i: (i, 0))
    vec = pl.BlockSpec((1, W), lambda i: (0, 0))
    return pl.pallas_call(
        functools.partial(_rwkv_fin_kernel, starts=starts),
        grid=(starts[-1],),
        in_specs=_part_specs(yf_parts, tm) + _part_specs(yb_parts, tm)
                 + [rows, rows, vec, vec, pl.BlockSpec((W, W), lambda i: (0, 0))],
        out_specs=rows,
        out_shape=jax.ShapeDtypeStruct((starts[-1] * tm, W), BF16),
        compiler_params=_params(("parallel",), 24 * tm * W * 4),
        name="rwkv_finalize",
    )(*yf_parts, *yb_parts, bonus, g, gn_g.reshape(1, W), gn_b.reshape(1, W),
      _head_sum_matrix(1.0 / RWKV_HEAD_DIM))


def _outproj_kernel(ac_ref, al_ref, fc_ref, fl_ref, rw_ref, w_ref, x_ref, g_ref, mod_ref, o_ref, *, starts):
    a0 = MLA_HEADS * MLA_V
    a1 = a0 + FNET_WIDTH

    def run(k):
        attn_ref, four_ref = ((ac_ref, fc_ref), (al_ref, fl_ref))[k]
        acc = _dot(attn_ref[...], w_ref[0:a0, :])
        acc += _dot(four_ref[...], w_ref[a0:a1, :])
        acc += _dot(rw_ref[...], w_ref[a1:, :])
        o_ref[...] = x_ref[...] + mod_ref[0, 2:3, :] * _rms(acc, g_ref[...])

    _for_own_part(starts, run)


def _out_projection(attn_parts, four_parts, rw, w_bf16, l, x, g, modg):
    R, D = x.shape
    tm = 512
    K = w_bf16.shape[1]
    return pl.pallas_call(
        functools.partial(_outproj_kernel, starts=_part_tiles(attn_parts, tm)),
        grid=(R // tm,),
        in_specs=_part_specs(attn_parts, tm) + _part_specs(four_parts, tm)
                 + [pl.BlockSpec((tm, rw.shape[1]), lambda i: (i, 0)),
                  pl.BlockSpec((None, K, D), lambda i: (l, 0, 0)),
                  pl.BlockSpec((tm, D), lambda i: (i, 0)),
                  pl.BlockSpec((1, D), lambda i: (0, 0)),
                  pl.BlockSpec((1, 6, D), lambda i: (i * (tm // ROW_GROUP), 0, 0))],
        out_specs=pl.BlockSpec((tm, D), lambda i: (i, 0)),
        out_shape=jax.ShapeDtypeStruct((R, D), F32),
        compiler_params=_params(("parallel",), 2 * K * D * 2 + 8 * tm * D * 4),
        name="out_projection",
    )(attn_parts[0], attn_parts[1], four_parts[0], four_parts[1], rw, w_bf16, x, g.reshape(1, D), modg)


FFN_SUB = 256
FFN_ACC_COLS = 512


def _ffn_kernel(x_ref, gpre_ref, mod_ref, wg_ref, wv_ref, cwg_ref, cwv_ref, cbg_ref, cbv_ref, wd_ref, gpost_ref,
                o_ref, a_scr, *, is_ctx_tile):
    j = pl.program_id(1)
    acc_scr = o_ref

    @pl.when(j == 0)
    def _():
        h = _rms(x_ref[...], gpre_ref[...]) * (1.0 + mod_ref[0, 4:5, :]) + mod_ref[0, 3:4, :]
        a_scr[...] = h.astype(BF16)
        acc_scr[...] = jnp.zeros_like(acc_scr)

    tm, D = acc_scr.shape
    tn = wg_ref.shape[1]
    seq = jnp.where(is_ctx_tile(pl.program_id(0)), SEQ, DEC_SEQ)
    pos = lax.broadcasted_iota(jnp.int32, (tm, FFN_SUB), 0) & (seq - 1)
    keep_prev = (pos != 0).astype(F32)
    keep_next = (pos != seq - 1).astype(F32)
    a = a_scr[...]

    def conv(u, cw_ref, cb_ref, cs):
        u_prev = pltpu.roll(u, 1, 0) * keep_prev
        u_next = pltpu.roll(u, tm - 1, 0) * keep_next
        return u_prev * cw_ref[0:1, cs] + u * cw_ref[1:2, cs] + u_next * cw_ref[2:3, cs] + cb_ref[:, cs]

    subs = [slice(s * FFN_SUB, (s + 1) * FFN_SUB) for s in range(tn // FFN_SUB)]
    ups = [(_dot(a, wg_ref[:, cs]), _dot(a, wv_ref[:, cs])) for cs in subs]
    for cs, (ug, uv) in zip(subs, ups):
        gate = conv(ug, cwg_ref, cbg_ref, cs)
        val = conv(uv, cwv_ref, cbv_ref, cs)
        act = (gate * _sigmoid(gate) * val).astype(BF16)
        for n in range(D // FFN_ACC_COLS):
            ns = slice(n * FFN_ACC_COLS, (n + 1) * FFN_ACC_COLS)
            acc_scr[:, ns] += _dot(act, wd_ref[cs, ns])

    @pl.when(j == pl.num_programs(1) - 1)
    def _():
        o_ref[...] = x_ref[...] + mod_ref[0, 5:6, :] * _rms(acc_scr[...], gpost_ref[...])


def _conv_ffn(x, g_pre, g_post, modg, w_up_bf16, w_down_bf16, l, conv_w, conv_b, n_ctx_rows, row0, n_rows):
    D = x.shape[1]
    tm, tn = DEC_SEQ, 2 * FFN_SUB
    nj = D_FF // tn
    t0 = row0 // tm
    n_ctx_tiles = n_ctx_rows // tm
    cb = conv_b.reshape(1, 2 * D_FF)
    return pl.pallas_call(
        functools.partial(_ffn_kernel, is_ctx_tile=lambda i: i + t0 < n_ctx_tiles),
        grid=(n_rows // tm, nj),
        in_specs=[pl.BlockSpec((tm, D), lambda i, j: (t0 + i, 0)),
                  pl.BlockSpec((1, D), lambda i, j: (0, 0)),
                  pl.BlockSpec((1, 6, D), lambda i, j: ((t0 + i) * (tm // ROW_GROUP), 0, 0)),
                  pl.BlockSpec((None, D, tn), lambda i, j: (l, 0, j)),
                  pl.BlockSpec((None, D, tn), lambda i, j: (l, 0, j + nj)),
                  pl.BlockSpec((3, tn), lambda i, j: (0, j)),
                  pl.BlockSpec((3, tn), lambda i, j: (0, j + nj)),
                  pl.BlockSpec((1, tn), lambda i, j: (0, j)),
                  pl.BlockSpec((1, tn), lambda i, j: (0, j + nj)),
                  pl.BlockSpec((None, tn, D), lambda i, j: (l, j, 0)),
                  pl.BlockSpec((1, D), lambda i, j: (0, 0))],
        out_specs=pl.BlockSpec((tm, D), lambda i, j: (i, 0), pipeline_mode=pl.Buffered(1)),
        out_shape=jax.ShapeDtypeStruct((n_rows, D), F32),
        scratch_shapes=[pltpu.VMEM((tm, D), BF16)],
        compiler_params=_params(("parallel", "arbitrary"),
                                3 * tm * D * 4 + tm * D * 2 + 6 * D * tn * 2 + 16 * tm * FFN_SUB * 4),
        name="conv_ffn",
    )(x, g_pre.reshape(1, D), modg, w_up_bf16, w_up_bf16, conv_w, conv_w, cb, cb, w_down_bf16,
      g_post.reshape(1, D))


def _layout_w_in(w):
    o = np.cumsum([0, Q_LORA, KV_LORA, MLA_ROPE, FNET_WIDTH, 3 * RWKV_WIDTH, W_LORA, A_LORA, G_LORA])
    q_dn, kv_dn, k_rope, xf, rkv, w_lo, a_lo, g_lo = [w[..., o[i]:o[i + 1]] for i in range(8)]
    zeros = jnp.zeros(w.shape[:-1] + (LANE - MLA_ROPE,), w.dtype)
    tail = jnp.zeros(w.shape[:-1] + (IN_PAD - COL_GLO - G_LORA,), w.dtype)
    return jnp.concatenate([rkv, q_dn, xf, kv_dn, k_rope, zeros, w_lo, a_lo, g_lo, tail], axis=-1).astype(BF16)


def _layout_w_uq(w):
    L = w.shape[0]
    w = w.reshape(L, Q_LORA, MLA_HEADS, MLA_NOPE + MLA_ROPE)
    w = jnp.pad(w, ((0, 0), (0, 0), (0, 0), (0, HEAD_PAD - MLA_NOPE - MLA_ROPE)))
    return w.reshape(L, Q_LORA, MLA_HEADS * HEAD_PAD).astype(BF16)


def _layout_w_ukv(w):
    L = w.shape[0]
    w = w.reshape(L, KV_LORA, MLA_HEADS, MLA_NOPE + MLA_V)
    wk = w[..., :MLA_NOPE].reshape(L, KV_LORA, MLA_HEADS * MLA_NOPE)
    wv = w[..., MLA_NOPE:].reshape(L, KV_LORA, MLA_HEADS * MLA_V)
    return wk.astype(BF16), wv.astype(BF16)


def _rope_table(tm):
    t = jnp.arange(DEC_SEQ)
    rowp = (t // GRID_W).astype(F32)
    colp = (t % GRID_W).astype(F32)
    n = MLA_ROPE // 4
    inv = ROPE_BASE ** (-jnp.arange(n, dtype=F32) / n)
    ang = jnp.concatenate([rowp[:, None] * inv, colp[:, None] * inv], axis=-1)
    cos = jnp.concatenate([jnp.ones((tm, MLA_ROPE // 2), F32), jnp.cos(ang)], axis=0)
    sin = jnp.concatenate([jnp.zeros((tm, MLA_ROPE // 2), F32), jnp.sin(ang)], axis=0)
    z32 = jnp.zeros_like(cos)
    z64 = jnp.zeros((cos.shape[0], LANE - MLA_ROPE), F32)
    mul = jnp.concatenate([cos, cos, z64], axis=1)
    left = jnp.concatenate([-sin, z32, z64], axis=1)
    right = jnp.concatenate([z32, sin, z64], axis=1)
    return jnp.stack([mul, left, right])


def kernel(x_prompt, x_sample, cache_mla_ckv, cache_mla_krope, state_rwkv, c, c_ctx, w_mod, b_mod, g_pre_mix, g_post_mix, g_pre_ffn, g_post_ffn, w_in, g_q_norm, w_uq, g_kv_norm, w_ukv, rwkv_conv, rwkv_w0, rwkv_w2, rwkv_a0, rwkv_a2, rwkv_g2, rwkv_k_k, rwkv_k_a, rwkv_r_k, rwkv_gn_g, rwkv_gn_b, w_out, ffn_w_up, ffn_conv, ffn_conv_b, ffn_w_down):
    Bc, Tc, D = x_prompt.shape
    Bl, Tl, _ = x_sample.shape
    L = w_mod.shape[0]
    assert (Tc, Tl, D) == (SEQ, DEC_SEQ, D_MODEL) and cache_mla_ckv.shape[2] == PAST_LEN
    assert Bl + 1 <= 8 and (Bc * Tc) % DEC_SEQ == 0
    n_ctx = Bc * Tc
    n_lat = Bl * Tl
    x = jnp.concatenate([x_prompt.reshape(n_ctx, D), x_sample.reshape(n_lat, D)], axis=0)

    cond = jnp.zeros((8, D), F32).at[0].set(c_ctx).at[1:1 + Bl].set(c)
    mods = _modulation(cond, w_mod, b_mod)
    group_row = np.concatenate([np.zeros(n_ctx // ROW_GROUP, np.int32),
                                1 + np.arange(n_lat // ROW_GROUP, dtype=np.int32) // (Tl // ROW_GROUP)])

    tm_mla = 512
    rope_tab = _rope_table(tm_mla)
    ridx = _rope_index(tm_mla, n_ctx)
    ridx_cache = lambda i: 0
    zero_state = jnp.zeros((Bc, 2, RWKV_HEADS, RWKV_HEAD_DIM, RWKV_HEAD_DIM), F32)

    w_in_b = _layout_w_in(w_in)
    w_uq_b = _layout_w_uq(w_uq)
    wk_b, wv_b = _layout_w_ukv(w_ukv)
    w_out_b = w_out.astype(BF16)
    w_up_b = ffn_w_up.astype(BF16)
    w_down_b = ffn_w_down.astype(BF16)

    ckv_out, krope_out, state_out = [], [], []
    for l in range(L):
        modg = mods[l][group_row].reshape(-1, 6, D)
        p = dict(rwkv_conv=rwkv_conv[l], rwkv_w0=rwkv_w0[l], rwkv_w2=rwkv_w2[l], rwkv_a0=rwkv_a0[l],
                 rwkv_a2=rwkv_a2[l], rwkv_g2=rwkv_g2[l], rwkv_k_k=rwkv_k_k[l], rwkv_k_a=rwkv_k_a[l],
                 rwkv_r_k=rwkv_r_k[l])

        hin = _in_projection(x, g_pre_mix[l], modg, w_in_b, l)

        q = _q_projection(hin, g_q_norm[l], w_uq_b, l, rope_tab, n_ctx)
        ckv, k, v = _kv_projection(hin, COL_KVDN // KV_LORA, COL_KROPE // LANE, g_kv_norm[l], wk_b, wv_b, l,
                                   rope_tab, ridx, tm_mla, True)
        cache_src = jnp.concatenate(
            [cache_mla_ckv[:, l].reshape(Bl * PAST_LEN, KV_LORA),
             jnp.pad(cache_mla_krope[:, l].reshape(Bl * PAST_LEN, MLA_ROPE), ((0, 0), (0, LANE - MLA_ROPE)))],
            axis=1)
        k_c, v_c = _kv_projection(cache_src, 0, KV_LORA // LANE, g_kv_norm[l], wk_b, wv_b, l,
                                  rope_tab, ridx_cache, PAST_LEN, False)
        attn = (_attention(q, 0, Bc, Tc, [(k, v, 0, Tc)]),
                _attention(q, n_ctx, Bl, Tl, [(k_c, v_c, 0, PAST_LEN), (k, v, n_ctx, Tl)]))

        four = (_fourier_mix(hin, 0, Bc, Tc), _fourier_mix(hin, n_ctx, Bl, Tl))

        prep = _rwkv_prep(hin, p, n_ctx)
        yf_ctx, yb_ctx, s_ctx = _rwkv_scan(prep, 0, Bc, Tc, zero_state)
        yf_lat, yb_lat, _ = _rwkv_scan(prep, n_ctx, Bl, Tl, state_rwkv[:, l])
        rw = _rwkv_finalize(list(yf_ctx) + list(yf_lat), list(yb_ctx) + list(yb_lat), prep[4], prep[3],
                            rwkv_gn_g[l], rwkv_gn_b[l])

        x = _out_projection(attn, four, rw, w_out_b, l, x, g_post_mix[l], modg)
        ffn = functools.partial(_conv_ffn, x, g_pre_ffn[l], g_post_ffn[l], modg, w_up_b, w_down_b, l,
                                ffn_conv[l], ffn_conv_b[l], n_ctx)
        if l + 1 < L:
            x = ffn(0, n_ctx + n_lat)
        else:
            y_prompt, y_sample = ffn(0, n_ctx), ffn(n_ctx, n_lat)

        ckv_out.append(ckv[:n_ctx].reshape(Bc, Tc, KV_LORA))
        krope_out.append(hin[:n_ctx, COL_KROPE:COL_KROPE + MLA_ROPE].reshape(Bc, Tc, MLA_ROPE))
        state_out.append(s_ctx)

    return (y_prompt.reshape(Bc, Tc, D), y_sample.reshape(Bl, Tl, D),
            jnp.stack(ckv_out, axis=1), jnp.stack(krope_out, axis=1), jnp.stack(state_out, axis=1))
```

```python
import functools
import math

import jax
import jax.numpy as jnp
import numpy as np
from jax import lax
from jax.experimental import pallas as pl
from jax.experimental.pallas import tpu as pltpu

F32 = jnp.float32
BF16 = jnp.bfloat16

D_MODEL = 2048
SEQ = 256
DEC_SEQ = 1024
PAST_LEN = 256
GRID_W = 64
MLA_HEADS = 8
MLA_NOPE = 128
MLA_ROPE = 64
MLA_V = 128
Q_LORA = 512
KV_LORA = 256
FNET_GROUPS = 4
FNET_GROUP_DIM = 128
FNET_WIDTH = FNET_GROUPS * FNET_GROUP_DIM
RWKV_HEADS = 8
RWKV_HEAD_DIM = 64
RWKV_WIDTH = RWKV_HEADS * RWKV_HEAD_DIM
W_LORA = 64
A_LORA = 64
G_LORA = 128
D_FF = 5632
ROPE_BASE = 10000.0
EPS = 1e-6
GN_EPS = 64e-5
DECAY_SCALE = math.exp(-0.5)

LANE = 128
ROW_GROUP = 256
HEAD_PAD = 256
CHUNK = 64
VMEM_CAP = 60 * 1024 * 1024

COL_RKV = 0
COL_QDN = 3 * RWKV_WIDTH
COL_XF = COL_QDN + Q_LORA
COL_KVDN = COL_XF + FNET_WIDTH
COL_KROPE = COL_KVDN + KV_LORA
COL_WA = COL_KROPE + LANE
COL_GLO = COL_WA + LANE
MXU_COLS = 256
IN_PAD = -(-(COL_GLO + G_LORA) // MXU_COLS) * MXU_COLS


SMALL_OPERANDS = 8 * 1024 * 1024


def _params(sem, nbytes):
    return pltpu.CompilerParams(dimension_semantics=sem,
                                vmem_limit_bytes=int(min(VMEM_CAP, nbytes + SMALL_OPERANDS)))


def _dot(a, b):
    return jnp.dot(a, b, preferred_element_type=F32)


def _dot_nt(a, b):
    return lax.dot_general(a, b, (((1,), (1,)), ((), ())), preferred_element_type=F32)


def _dot_tn(a, b):
    return lax.dot_general(a, b, (((0,), (0,)), ((), ())), preferred_element_type=F32)


def _split_bf16(x, n):
    terms = []
    rem = x
    for i in range(n):
        t = rem.astype(BF16)
        terms.append(t)
        if i + 1 < n:
            rem = rem - t.astype(F32)
    return terms


def _dot_exact_rhs(a, b_exact_bf16, n):
    acc = None
    for t in _split_bf16(a, n):
        p = _dot(t, b_exact_bf16)
        acc = p if acc is None else acc + p
    return acc


def _rms(x, g):
    return x * lax.rsqrt(jnp.mean(x * x, axis=-1, keepdims=True) + EPS) * g


def _sigmoid(x):
    return 1.0 / (1.0 + jnp.exp(-x))


def _mod_kernel(cond_ref, w_ref, b_ref, o_ref):
    c = cond_ref[...]
    s = (c * _sigmoid(c)).astype(BF16)
    o_ref[0] = _dot(s, w_ref[0].astype(BF16)) + b_ref[0]


def _modulation(cond, w_mod, b_mod):
    L, D, N = w_mod.shape
    tn = 2048
    return pl.pallas_call(
        _mod_kernel,
        grid=(L, N // tn),
        in_specs=[pl.BlockSpec((8, D), lambda l, j: (0, 0)),
                  pl.BlockSpec((1, D, tn), lambda l, j: (l, 0, j)),
                  pl.BlockSpec((1, 1, tn), lambda l, j: (l, 0, j))],
        out_specs=pl.BlockSpec((1, 8, tn), lambda l, j: (l, 0, j)),
        out_shape=jax.ShapeDtypeStruct((L, 8, N), F32),
        compiler_params=_params(("parallel", "parallel"), 3 * D * tn * 4),
        name="modulation",
    )(cond, w_mod, b_mod.reshape(L, 1, N))


def _inproj_kernel(x_ref, g_ref, mod_ref, w_ref, o_ref):
    h = _rms(x_ref[...], g_ref[...]) * (1.0 + mod_ref[0, 1:2, :]) + mod_ref[0, 0:1, :]
    o_ref[...] = _dot(h.astype(BF16), w_ref[...])


def _in_projection(x, g, modg, w_bf16, l):
    R, D = x.shape
    N = w_bf16.shape[2]
    tm = 512
    return pl.pallas_call(
        _inproj_kernel,
        grid=(R // tm,),
        in_specs=[pl.BlockSpec((tm, D), lambda i: (i, 0)),
                  pl.BlockSpec((1, D), lambda i: (0, 0)),
                  pl.BlockSpec((1, 6, D), lambda i: (i * (tm // ROW_GROUP), 0, 0)),
                  pl.BlockSpec((None, D, N), lambda i: (l, 0, 0), pipeline_mode=pl.Buffered(1))],
        out_specs=pl.BlockSpec((tm, N), lambda i: (i, 0)),
        out_shape=jax.ShapeDtypeStruct((R, N), F32),
        compiler_params=_params(("parallel",), 2 * tm * D * 4 + tm * D * 2 + D * N * 2 + 3 * tm * N * 4),
        name="in_projection",
    )(x, g.reshape(1, D), modg, w_bf16)


def _rope_block(blk, rope_ref):
    return (blk * rope_ref[0]
            + pltpu.roll(blk, LANE - MLA_ROPE // 2, 1) * rope_ref[1]
            + pltpu.roll(blk, MLA_ROPE // 2, 1) * rope_ref[2])


def _q_kernel(qdn_ref, g_ref, w_ref, rope_ref, o_ref):
    scale = (MLA_NOPE + MLA_ROPE) ** -0.5
    qn = _rms(qdn_ref[...], g_ref[...]).astype(BF16)
    q = _dot(qn, w_ref[...]) * scale
    parts = []
    for h in range(MLA_HEADS):
        base = h * HEAD_PAD
        parts.append(q[:, base:base + MLA_NOPE])
        parts.append(_rope_block(q[:, base + MLA_NOPE:base + HEAD_PAD], rope_ref))
    o_ref[...] = jnp.concatenate(parts, axis=1).astype(o_ref.dtype)


def _rope_index(tm, n_ctx_rows):
    n_ctx_tiles = n_ctx_rows // tm
    per_seq = DEC_SEQ // tm

    def index(i):
        return jnp.where(i < n_ctx_tiles, 0, 1 + (i - n_ctx_tiles) % per_seq)
    return index


def _q_projection(hin, g, w_bf16, l, rope_tab, n_ctx_rows):
    R = hin.shape[0]
    tm = 512
    N = MLA_HEADS * HEAD_PAD
    ridx = _rope_index(tm, n_ctx_rows)
    return pl.pallas_call(
        _q_kernel,
        grid=(R // tm,),
        in_specs=[pl.BlockSpec((tm, Q_LORA), lambda i: (i, COL_QDN // Q_LORA)),
                  pl.BlockSpec((1, Q_LORA), lambda i: (0, 0)),
                  pl.BlockSpec((None, Q_LORA, N), lambda i: (l, 0, 0)),
                  pl.BlockSpec((3, tm, LANE), lambda i: (0, ridx(i), 0))],
        out_specs=pl.BlockSpec((tm, N), lambda i: (i, 0)),
        out_shape=jax.ShapeDtypeStruct((R, N), BF16),
        compiler_params=_params(("parallel",), 2 * Q_LORA * N * 2 + 6 * tm * N * 4),
        name="q_projection",
    )(hin, g.reshape(1, Q_LORA), w_bf16, rope_tab)


def _kv_kernel(kvdn_ref, kr_ref, g_ref, wk_ref, wv_ref, rope_ref, *out_refs, normalize):
    k_ref, v_ref = out_refs[-2:]
    c = kvdn_ref[...]
    if normalize:
        c = _rms(c, g_ref[...])
        out_refs[0][...] = c
    cb = c.astype(BF16)
    kn = _dot(cb, wk_ref[...])
    v_ref[...] = _dot(cb, wv_ref[...]).astype(v_ref.dtype)
    kr = _rope_block(kr_ref[...], rope_ref)
    parts = []
    for h in range(MLA_HEADS):
        parts.append(kn[:, h * MLA_NOPE:(h + 1) * MLA_NOPE])
        parts.append(kr)
    k_ref[...] = jnp.concatenate(parts, axis=1).astype(k_ref.dtype)


def _kv_projection(src, kv_col, kr_col, g, wk_bf16, wv_bf16, l, rope_tab, ridx, tm, normalize):
    R = src.shape[0]
    NK = MLA_HEADS * HEAD_PAD
    NV = MLA_HEADS * MLA_V
    return pl.pallas_call(
        functools.partial(_kv_kernel, normalize=normalize),
        grid=(R // tm,),
        in_specs=[pl.BlockSpec((tm, KV_LORA), lambda i: (i, kv_col)),
                  pl.BlockSpec((tm, LANE), lambda i: (i, kr_col)),
                  pl.BlockSpec((1, KV_LORA), lambda i: (0, 0)),
                  pl.BlockSpec((None, KV_LORA, MLA_HEADS * MLA_NOPE), lambda i: (l, 0, 0)),
                  pl.BlockSpec((None, KV_LORA, NV), lambda i: (l, 0, 0)),
                  pl.BlockSpec((3, tm, LANE), lambda i: (0, ridx(i), 0))],
        out_specs=([pl.BlockSpec((tm, KV_LORA), lambda i: (i, 0))] if normalize else [])
                  + [pl.BlockSpec((tm, NK), lambda i: (i, 0)), pl.BlockSpec((tm, NV), lambda i: (i, 0))],
        out_shape=([jax.ShapeDtypeStruct((R, KV_LORA), F32)] if normalize else [])
                  + [jax.ShapeDtypeStruct((R, NK), BF16), jax.ShapeDtypeStruct((R, NV), BF16)],
        compiler_params=_params(("parallel",), 8 * tm * NK * 4),
        name="kv_projection" if normalize else "kv_projection_cache",
    )(src, src, g.reshape(1, KV_LORA), wk_bf16, wv_bf16, rope_tab)


def _attn_kernel(q_ref, *refs):
    o_ref = refs[-1]
    kv = [(refs[i], refs[i + 1]) for i in range(0, len(refs) - 1, 2)]
    for h in range(MLA_HEADS):
        qk = slice(h * HEAD_PAD, (h + 1) * HEAD_PAD)
        vo = slice(h * MLA_V, (h + 1) * MLA_V)
        q = q_ref[:, qk]
        scores = [_dot_nt(q, k_ref[:, qk]) for k_ref, _ in kv]
        m = functools.reduce(jnp.maximum, [jnp.max(s, axis=-1, keepdims=True) for s in scores])
        probs = [jnp.exp(s - m) for s in scores]
        l = functools.reduce(jnp.add, [jnp.sum(p, axis=-1, keepdims=True) for p in probs])
        o = functools.reduce(jnp.add, [_dot(p.astype(BF16), v_ref[:, vo]) for p, (_, v_ref) in zip(probs, kv)])
        o_ref[:, vo] = (o / l).astype(o_ref.dtype)


def _attention(q, q_row0, n_batch, tq_total, sources):
    tq = min(512, tq_total)
    nq = tq_total // tq
    q0 = q_row0 // tq
    NQ = MLA_HEADS * HEAD_PAD
    NV = MLA_HEADS * MLA_V
    in_specs = [pl.BlockSpec((tq, NQ), lambda b, i: (q0 + b * nq + i, 0))]
    args = [q]
    tk_total = 0
    for k, v, row0, tk in sources:
        in_specs.append(pl.BlockSpec((tk, NQ), lambda b, i, o=row0 // tk: (o + b, 0)))
        in_specs.append(pl.BlockSpec((tk, NV), lambda b, i, o=row0 // tk: (o + b, 0)))
        args += [k, v]
        tk_total += tk
    return pl.pallas_call(
        _attn_kernel,
        grid=(n_batch, nq),
        in_specs=in_specs,
        out_specs=pl.BlockSpec((tq, NV), lambda b, i: (b * nq + i, 0)),
        out_shape=jax.ShapeDtypeStruct((n_batch * tq_total, NV), BF16),
        compiler_params=_params(("parallel", "arbitrary"),
                                2 * (tq + tk_total) * (NQ + NV) * 2 + 6 * tq * tk_total * 4),
        name="attention",
    )(*args)


def _fnet_kernel(z_ref, cs_ref, ct_ref, o_ref):
    yc, ys = [], []
    for g in range(FNET_GROUPS):
        zg = z_ref[:, g * FNET_GROUP_DIM:(g + 1) * FNET_GROUP_DIM].astype(BF16)
        y = _dot(zg, cs_ref[...])
        yc.append(y[:, :FNET_GROUP_DIM])
        ys.append(y[:, FNET_GROUP_DIM:])
    stacked = jnp.concatenate([jnp.concatenate(yc, axis=1), jnp.concatenate(ys, axis=1)], axis=0)
    o_ref[...] = _dot(ct_ref[...], stacked.astype(BF16)).astype(o_ref.dtype)


def _dft_tables(T):
    def cs(n):
        i = np.arange(n, dtype=np.int64)
        ang = ((i[:, None] * i[None, :]) % n).astype(np.float64) * (2.0 * math.pi / n)
        return np.cos(ang), np.sin(ang)
    cc, sc = cs(FNET_GROUP_DIM)
    ct, st = cs(T)
    norm = 1.0 / math.sqrt(T * FNET_GROUP_DIM)
    return (jnp.asarray(np.concatenate([cc, sc], axis=1), F32).astype(BF16),
            jnp.asarray(np.concatenate([ct, -st], axis=1) * norm, F32).astype(BF16))


def _fourier_mix(hin, row0, n_batch, T):
    cs_tab, ct_tab = _dft_tables(T)
    b0 = row0 // T
    return pl.pallas_call(
        _fnet_kernel,
        grid=(n_batch,),
        in_specs=[pl.BlockSpec((T, FNET_WIDTH), lambda b: (b0 + b, COL_XF // FNET_WIDTH)),
                  pl.BlockSpec((FNET_GROUP_DIM, 2 * FNET_GROUP_DIM), lambda b: (0, 0)),
                  pl.BlockSpec((T, 2 * T), lambda b: (0, 0))],
        out_specs=pl.BlockSpec((T, FNET_WIDTH), lambda b: (b, 0)),
        out_shape=jax.ShapeDtypeStruct((n_batch * T, FNET_WIDTH), BF16),
        compiler_params=_params(("parallel",), 2 * T * 2 * T * 2 + 10 * T * FNET_WIDTH * 4),
        name="fourier_mix",
    )(hin, cs_tab, ct_tab)


def _head_sum_matrix(scale):
    i = np.arange(RWKV_WIDTH) // RWKV_HEAD_DIM
    return jnp.asarray((i[:, None] == i[None, :]).astype(np.float32) * scale, BF16)


def _rwkv_prep_kernel(x_ref, xp_ref, xn_ref, wa_ref, gl_ref, cw_ref, w0_ref, w2_ref, a0_ref, a2_ref, g2_ref,
                      kk_ref, ka_ref, rk_ref, hs_ref,
                      r_ref, kh_ref, v_ref, g_ref, bonus_ref, lw_ref, kt_ref, beta_ref,
                      *, n_ctx_tiles, tiles_per_seq):
    i = pl.program_id(0)
    j = i - n_ctx_tiles
    first = jnp.logical_or(i < n_ctx_tiles, j % tiles_per_seq == 0)
    last = jnp.logical_or(i < n_ctx_tiles, j % tiles_per_seq == tiles_per_seq - 1)
    x = x_ref[...]
    tm = x.shape[0]
    row = lax.broadcasted_iota(jnp.int32, x.shape, 0)
    prev_row = jnp.where(first, 0.0, xp_ref[7:8, :])
    next_row = jnp.where(last, 0.0, xn_ref[0:1, :])
    x_prev = jnp.where(row == 0, prev_row, pltpu.roll(x, 1, 0))
    x_next = jnp.where(row == tm - 1, next_row, pltpu.roll(x, tm - 1, 0))
    xc = x_prev * cw_ref[0:1, :] + x * cw_ref[1:2, :] + x_next * cw_ref[2:3, :]
    R_ = RWKV_WIDTH
    r, k, v = xc[:, :R_], xc[:, R_:2 * R_], xc[:, 2 * R_:]
    wl = jnp.tanh(wa_ref[:, :W_LORA]).astype(BF16)
    al = wa_ref[:, W_LORA:W_LORA + A_LORA].astype(BF16)
    g_ref[...] = _dot(_sigmoid(gl_ref[...]).astype(BF16), g2_ref[...].astype(BF16))
    kappa = k * kk_ref[...]
    ss = _dot_exact_rhs(kappa * kappa, hs_ref[...], 3)
    kh = kappa * lax.rsqrt(ss + EPS)
    kt_sum = None
    for d in range(2):
        lw_ref[d] = -DECAY_SCALE * _sigmoid(w0_ref[d:d + 1, :] + _dot(wl, w2_ref[d].astype(BF16)))
        a = _sigmoid(a0_ref[d:d + 1, :] + _dot(al, a2_ref[d].astype(BF16)))
        kt = k * (1.0 + (a - 1.0) * ka_ref[...])
        kt_ref[d] = kt.astype(kt_ref.dtype)
        beta_ref[d] = (a * kh).astype(beta_ref.dtype)
        kt_sum = kt if kt_sum is None else kt_sum + kt
    r_ref[...] = r.astype(r_ref.dtype)
    kh_ref[...] = kh.astype(kh_ref.dtype)
    v_ref[...] = v.astype(v_ref.dtype)
    bonus_ref[...] = _dot_exact_rhs(r * kt_sum * rk_ref[...], hs_ref[...], 3) * v


def _rwkv_prep(hin, p, n_ctx_rows):
    R = hin.shape[0]
    tm = ROW_GROUP
    W3 = 3 * RWKV_WIDTH
    nh = tm // 8
    last_blk = R // 8 - 1
    row = lambda a: a.reshape(1, RWKV_WIDTH)
    full = lambda *s: pl.BlockSpec(s, lambda i: (0,) * len(s))
    out_rw = lambda dt: jax.ShapeDtypeStruct((R, RWKV_WIDTH), dt)
    out_rw2 = lambda dt: jax.ShapeDtypeStruct((2, R, RWKV_WIDTH), dt)
    spec_rw = pl.BlockSpec((tm, RWKV_WIDTH), lambda i: (i, 0))
    spec_rw2 = pl.BlockSpec((2, tm, RWKV_WIDTH), lambda i: (0, i, 0))
    return pl.pallas_call(
        functools.partial(_rwkv_prep_kernel, n_ctx_tiles=n_ctx_rows // tm, tiles_per_seq=DEC_SEQ // tm),
        grid=(R // tm,),
        in_specs=[pl.BlockSpec((tm, W3), lambda i: (i, COL_RKV // W3)),
                  pl.BlockSpec((8, W3), lambda i: (jnp.maximum(i * nh - 1, 0), 0)),
                  pl.BlockSpec((8, W3), lambda i: (jnp.minimum((i + 1) * nh, last_blk), 0)),
                  pl.BlockSpec((tm, LANE), lambda i: (i, COL_WA // LANE)),
                  pl.BlockSpec((tm, G_LORA), lambda i: (i, COL_GLO // G_LORA)),
                  full(3, W3), full(2, RWKV_WIDTH), full(2, W_LORA, RWKV_WIDTH),
                  full(2, RWKV_WIDTH), full(2, A_LORA, RWKV_WIDTH), full(G_LORA, RWKV_WIDTH),
                  full(1, RWKV_WIDTH), full(1, RWKV_WIDTH), full(1, RWKV_WIDTH),
                  full(RWKV_WIDTH, RWKV_WIDTH)],
        out_specs=[spec_rw] * 5 + [spec_rw2] * 3,
        out_shape=[out_rw(BF16)] * 3 + [out_rw(F32)] * 2 + [out_rw2(F32), out_rw2(BF16), out_rw2(BF16)],
        compiler_params=_params(("parallel",), 40 * tm * W3 * 4),
        name="rwkv_prep",
    )(hin, hin, hin, hin, hin, p['rwkv_conv'], p['rwkv_w0'], p['rwkv_w2'], p['rwkv_a0'], p['rwkv_a2'],
      p['rwkv_g2'], row(p['rwkv_k_k']), row(p['rwkv_k_a']), row(p['rwkv_r_k']), _head_sum_matrix(1.0))


def _rwkv_scan_kernel(*refs, n_seq):
    seq_refs = [refs[13 * p:13 * (p + 1)] for p in range(n_seq)]
    mi_ref = refs[13 * n_seq]
    out_refs = refs[13 * n_seq + 1:-1]
    s_scr = refs[-1]
    c = pl.program_id(1)
    nc = pl.num_programs(1)
    N = RWKV_HEAD_DIM
    C = CHUNK
    H = RWKV_HEADS

    @pl.when(c == 0)
    def _():
        for p in range(n_seq):
            s_scr[p] = seq_refs[p][12][0]

    eye = (lax.broadcasted_iota(jnp.int32, (C, C), 0) == lax.broadcasted_iota(jnp.int32, (C, C), 1)).astype(F32)
    dirs = [(p, d, tuple(sr[3 * d:3 * d + 3]) + tuple(sr[6 + 3 * d:9 + 3 * d]))
            for p, sr in enumerate(seq_refs) for d in range(2)]
    chains = []
    for p, d, (r_ref, kh_ref, v_ref, lw_ref, kt_ref, beta_ref) in dirs:
        mi = mi_ref[d]
        lw = lw_ref[0]
        cl = _dot_exact_lhs(mi.astype(BF16), lw, 3)
        cl_tot = jnp.sum(lw, axis=0, keepdims=True)
        g_inv = jnp.exp(-cl)
        g_end = jnp.exp(cl_tot - cl)
        g_tot = jnp.exp(cl_tot)
        kh, beta, kt = kh_ref[...], beta_ref[0], kt_ref[0]
        k_dec = kh * jnp.exp(cl - lw)
        r_dec = r_ref[...] * jnp.exp(cl)
        b_inv, k_inv = beta * g_inv, kt * g_inv
        k_end, b_end = kt * g_end, beta * g_end
        v_all = v_ref[...]
        for h in range(H):
            sl = slice(h * N, (h + 1) * N)
            chains.append(dict(p=p, d=d, h=h, mi=mi, ms=mi - eye, kd=k_dec[:, sl], rd=r_dec[:, sl],
                               bi=b_inv[:, sl], ki=k_inv[:, sl], ke=k_end[:, sl], be=b_end[:, sl],
                               v=v_all[:, sl], gt=g_tot[:, sl], s0=s_scr[p, d, h]))

    for ch in chains:
        lhs = jnp.concatenate([ch['kd'], ch['rd']], axis=0).astype(BF16)
        rhs = jnp.concatenate([ch['bi'], ch['ki']], axis=0).astype(BF16)
        nt = _dot_nt(lhs, rhs)
        ch['a'] = nt[:C, :C] * ch['ms']
        ch['b'] = (nt[:C, C:] * ch['ms']).astype(BF16)
        ch['qp'] = (nt[C:, :] * jnp.concatenate([-ch['mi'], ch['mi']], axis=1)).astype(BF16)
        ch['lhs'] = lhs
        ch['t'] = eye - ch['a']
    for ch in chains:
        xb = (-ch['a']).astype(BF16)
        ch['xp'] = _dot(xb, xb).astype(BF16)
    n_sq = int(math.log2(C)) - 1
    for k in range(n_sq):
        last = k == n_sq - 1
        for ch in chains:
            tb = ch['t'].astype(BF16)
            prod = _dot(tb if last else jnp.concatenate([tb, ch['xp']], axis=0), ch['xp'])
            ch['t'] = ch['t'] + prod[:C]
            if not last:
                ch['xp'] = prod[C:].astype(BF16)
    for ch in chains:
        ch['s0b'] = ch['s0'].astype(BF16)
        ch['vb'] = ch['v'].astype(BF16)
        ks = _dot_nt(ch['lhs'], ch['s0b'])
        ch['rs'] = ks[C:]
        ch['z'] = ks[:C] + _dot(ch['b'], ch['vb'])
    for ch in chains:
        u = _dot(ch['t'].astype(BF16), ch['z'].astype(BF16))
        ch['uv'] = jnp.concatenate([u.astype(BF16), ch['vb']], axis=0)
    for ch in chains:
        kb = jnp.concatenate([-ch['be'], ch['ke']], axis=0).astype(BF16)
        s_scr[ch['p'], ch['d'], ch['h']] = ch['s0'] * ch['gt'] + _dot_tn(ch['uv'], kb)
    for ch in chains:
        ch['y'] = ch['rs'] + _dot(ch['qp'], ch['uv'])
    for p in range(n_seq):
        for d in range(2):
            ys = [ch['y'] for ch in chains[(2 * p + d) * H:(2 * p + d + 1) * H]]
            out_refs[3 * p + d][...] = jnp.concatenate(ys, axis=1)

    @pl.when(c == nc - 1)
    def _():
        for p in range(n_seq):
            out_refs[3 * p + 2][0] = s_scr[p]


def _dot_exact_lhs(a_exact_bf16, b, n):
    acc = None
    for t in _split_bf16(b, n):
        p = _dot(a_exact_bf16, t)
        acc = p if acc is None else acc + p
    return acc


def _causal_masks():
    t = np.arange(CHUNK)
    fwd = (t[None, :] <= t[:, None]).astype(np.float32)
    return jnp.asarray(np.stack([fwd, fwd.T]))


def _rwkv_scan(prep, row0, n_batch, T, s0):
    r, kh, v, _, _, lw, kt, beta = prep
    C = CHUNK
    nc = T // C
    c0 = row0 // C
    W = RWKV_WIDTH
    n_seq = 2 if n_batch % 2 == 0 else 1
    nb = n_batch // n_seq
    st_shape = (1, 2, RWKV_HEADS, RWKV_HEAD_DIM, RWKV_HEAD_DIM)
    in_specs, args, out_specs, out_shape = [], [], [], []
    for p in range(n_seq):
        fwd = lambda b, c, p=p: (p * nb + b) * nc + c
        bwd = lambda b, c, p=p: (p * nb + b) * nc + nc - 1 - c
        shared_f = pl.BlockSpec((C, W), lambda b, c, f=fwd: (c0 + f(b, c), 0))
        shared_b = pl.BlockSpec((C, W), lambda b, c, f=bwd: (c0 + f(b, c), 0))
        dir_f = pl.BlockSpec((1, C, W), lambda b, c, f=fwd: (0, c0 + f(b, c), 0))
        dir_b = pl.BlockSpec((1, C, W), lambda b, c, f=bwd: (1, c0 + f(b, c), 0))
        in_specs += [shared_f] * 3 + [shared_b] * 3 + [dir_f] * 3 + [dir_b] * 3
        in_specs.append(pl.BlockSpec(st_shape, lambda b, c, p=p: (p * nb + b, 0, 0, 0, 0)))
        args += [r, kh, v, r, kh, v, lw, kt, beta, lw, kt, beta, s0]
        out_specs += [pl.BlockSpec((C, W), lambda b, c: (b * nc + c, 0)),
                      pl.BlockSpec((C, W), lambda b, c: (b * nc + nc - 1 - c, 0)),
                      pl.BlockSpec(st_shape, lambda b, c: (b, 0, 0, 0, 0))]
        out_shape += [jax.ShapeDtypeStruct((nb * T, W), F32)] * 2
        out_shape.append(jax.ShapeDtypeStruct((nb,) + st_shape[1:], F32))
    in_specs.append(pl.BlockSpec((2, C, C), lambda b, c: (0, 0, 0)))
    outs = pl.pallas_call(
        functools.partial(_rwkv_scan_kernel, n_seq=n_seq),
        grid=(nb, nc),
        in_specs=in_specs,
        out_specs=out_specs,
        out_shape=out_shape,
        scratch_shapes=[pltpu.VMEM((n_seq, 2, RWKV_HEADS, RWKV_HEAD_DIM, RWKV_HEAD_DIM), F32)],
        compiler_params=_params(("parallel", "arbitrary"), 32 * 1024 * 1024),
        name="rwkv_scan",
    )(*args, _causal_masks())
    return outs[0::3], outs[1::3], jnp.concatenate(outs[2::3], axis=0)


def _part_tiles(parts, tm):
    return [int(t) for t in np.cumsum([0] + [p.shape[0] // tm for p in parts])]


def _part_specs(parts, tm):
    starts = _part_tiles(parts, tm)
    return [pl.BlockSpec((tm, p.shape[1]), lambda i, t0=t0, n=t1 - t0: (jnp.clip(i - t0, 0, n - 1), 0))
            for p, t0, t1 in zip(parts, starts[:-1], starts[1:])]


def _for_own_part(starts, run):
    i = pl.program_id(0)
    for k, (t0, t1) in enumerate(zip(starts[:-1], starts[1:])):
        pl.when(jnp.logical_and(i >= t0, i < t1))(functools.partial(run, k))


def _rwkv_fin_kernel(*refs, starts):
    n = len(starts) - 1
    yf_refs, yb_refs = refs[:n], refs[n:2 * n]
    bonus_ref, g_ref, gg_ref, gb_ref, hm_ref, o_ref = refs[2 * n:]

    def run(k):
        y = yf_refs[k][...] + yb_refs[k][...]
        mu = _dot_exact_rhs(y, hm_ref[...], 3)
        yc = y - mu
        var = _dot_exact_rhs(yc * yc, hm_ref[...], 3)
        yn = yc * lax.rsqrt(var + GN_EPS) * gg_ref[...] + gb_ref[...]
        o_ref[...] = ((yn + bonus_ref[...]) * g_ref[...]).astype(o_ref.dtype)

    _for_own_part(starts, run)


def _rwkv_finalize(yf_parts, yb_parts, bonus, g, gn_g, gn_b):
    tm = 512
    W = RWKV_WIDTH
    starts = _part_tiles(yf_parts, tm)
    rows = pl.BlockSpec((tm, W), lambda i: (i, 0))
    vec = pl.BlockSpec((1, W), lambda i: (0, 0))
    return pl.pallas_call(
        functools.partial(_rwkv_fin_kernel, starts=starts),
        grid=(starts[-1],),
        in_specs=_part_specs(yf_parts, tm) + _part_specs(yb_parts, tm)
                 + [rows, rows, vec, vec, pl.BlockSpec((W, W), lambda i: (0, 0))],
        out_specs=rows,
        out_shape=jax.ShapeDtypeStruct((starts[-1] * tm, W), BF16),
        compiler_params=_params(("parallel",), 24 * tm * W * 4),
        name="rwkv_finalize",
    )(*yf_parts, *yb_parts, bonus, g, gn_g.reshape(1, W), gn_b.reshape(1, W),
      _head_sum_matrix(1.0 / RWKV_HEAD_DIM))


def _outproj_kernel(ac_ref, al_ref, fc_ref, fl_ref, rw_ref, w_ref, x_ref, g_ref, mod_ref, o_ref, *, starts):
    a0 = MLA_HEADS * MLA_V
    a1 = a0 + FNET_WIDTH

    def run(k):
        attn_ref, four_ref = ((ac_ref, fc_ref), (al_ref, fl_ref))[k]
        acc = _dot(attn_ref[...], w_ref[0:a0, :])
        acc += _dot(four_ref[...], w_ref[a0:a1, :])
        acc += _dot(rw_ref[...], w_ref[a1:, :])
        o_ref[...] = x_ref[...] + mod_ref[0, 2:3, :] * _rms(acc, g_ref[...])

    _for_own_part(starts, run)


def _out_projection(attn_parts, four_parts, rw, w_bf16, l, x, g, modg):
    R, D = x.shape
    tm = 512
    K = w_bf16.shape[1]
    return pl.pallas_call(
        functools.partial(_outproj_kernel, starts=_part_tiles(attn_parts, tm)),
        grid=(R // tm,),
        in_specs=_part_specs(attn_parts, tm) + _part_specs(four_parts, tm)
                 + [pl.BlockSpec((tm, rw.shape[1]), lambda i: (i, 0)),
                  pl.BlockSpec((None, K, D), lambda i: (l, 0, 0)),
                  pl.BlockSpec((tm, D), lambda i: (i, 0)),
                  pl.BlockSpec((1, D), lambda i: (0, 0)),
                  pl.BlockSpec((1, 6, D), lambda i: (i * (tm // ROW_GROUP), 0, 0))],
        out_specs=pl.BlockSpec((tm, D), lambda i: (i, 0)),
        out_shape=jax.ShapeDtypeStruct((R, D), F32),
        compiler_params=_params(("parallel",), 2 * K * D * 2 + 8 * tm * D * 4),
        name="out_projection",
    )(attn_parts[0], attn_parts[1], four_parts[0], four_parts[1], rw, w_bf16, x, g.reshape(1, D), modg)


FFN_SUB = 256
FFN_ACC_COLS = 512


def _ffn_kernel(x_ref, gpre_ref, mod_ref, wg_ref, wv_ref, cwg_ref, cwv_ref, cbg_ref, cbv_ref, wd_ref, gpost_ref,
                o_ref, a_scr, *, is_ctx_tile):
    j = pl.program_id(1)
    acc_scr = o_ref

    @pl.when(j == 0)
    def _():
        h = _rms(x_ref[...], gpre_ref[...]) * (1.0 + mod_ref[0, 4:5, :]) + mod_ref[0, 3:4, :]
        a_scr[...] = h.astype(BF16)
        acc_scr[...] = jnp.zeros_like(acc_scr)

    tm, D = acc_scr.shape
    tn = wg_ref.shape[1]
    seq = jnp.where(is_ctx_tile(pl.program_id(0)), SEQ, DEC_SEQ)
    pos = lax.broadcasted_iota(jnp.int32, (tm, FFN_SUB), 0) & (seq - 1)
    keep_prev = (pos != 0).astype(F32)
    keep_next = (pos != seq - 1).astype(F32)
    a = a_scr[...]

    def conv(u, cw_ref, cb_ref, cs):
        u_prev = pltpu.roll(u, 1, 0) * keep_prev
        u_next = pltpu.roll(u, tm - 1, 0) * keep_next
        return u_prev * cw_ref[0:1, cs] + u * cw_ref[1:2, cs] + u_next * cw_ref[2:3, cs] + cb_ref[:, cs]

    subs = [slice(s * FFN_SUB, (s + 1) * FFN_SUB) for s in range(tn // FFN_SUB)]
    ups = [(_dot(a, wg_ref[:, cs]), _dot(a, wv_ref[:, cs])) for cs in subs]
    for cs, (ug, uv) in zip(subs, ups):
        gate = conv(ug, cwg_ref, cbg_ref, cs)
        val = conv(uv, cwv_ref, cbv_ref, cs)
        act = (gate * _sigmoid(gate) * val).astype(BF16)
        for n in range(D // FFN_ACC_COLS):
            ns = slice(n * FFN_ACC_COLS, (n + 1) * FFN_ACC_COLS)
            acc_scr[:, ns] += _dot(act, wd_ref[cs, ns])

    @pl.when(j == pl.num_programs(1) - 1)
    def _():
        o_ref[...] = x_ref[...] + mod_ref[0, 5:6, :] * _rms(acc_scr[...], gpost_ref[...])


def _conv_ffn(x, g_pre, g_post, modg, w_up_bf16, w_down_bf16, l, conv_w, conv_b, n_ctx_rows, row0, n_rows):
    D = x.shape[1]
    tm, tn = DEC_SEQ, 2 * FFN_SUB
    nj = D_FF // tn
    t0 = row0 // tm
    n_ctx_tiles = n_ctx_rows // tm
    cb = conv_b.reshape(1, 2 * D_FF)
    return pl.pallas_call(
        functools.partial(_ffn_kernel, is_ctx_tile=lambda i: i + t0 < n_ctx_tiles),
        grid=(n_rows // tm, nj),
        in_specs=[pl.BlockSpec((tm, D), lambda i, j: (t0 + i, 0)),
                  pl.BlockSpec((1, D), lambda i, j: (0, 0)),
                  pl.BlockSpec((1, 6, D), lambda i, j: ((t0 + i) * (tm // ROW_GROUP), 0, 0)),
                  pl.BlockSpec((None, D, tn), lambda i, j: (l, 0, j)),
                  pl.BlockSpec((None, D, tn), lambda i, j: (l, 0, j + nj)),
                  pl.BlockSpec((3, tn), lambda i, j: (0, j)),
                  pl.BlockSpec((3, tn), lambda i, j: (0, j + nj)),
                  pl.BlockSpec((1, tn), lambda i, j: (0, j)),
                  pl.BlockSpec((1, tn), lambda i, j: (0, j + nj)),
                  pl.BlockSpec((None, tn, D), lambda i, j: (l, j, 0)),
                  pl.BlockSpec((1, D), lambda i, j: (0, 0))],
        out_specs=pl.BlockSpec((tm, D), lambda i, j: (i, 0), pipeline_mode=pl.Buffered(1)),
        out_shape=jax.ShapeDtypeStruct((n_rows, D), F32),
        scratch_shapes=[pltpu.VMEM((tm, D), BF16)],
        compiler_params=_params(("parallel", "arbitrary"),
                                3 * tm * D * 4 + tm * D * 2 + 6 * D * tn * 2 + 16 * tm * FFN_SUB * 4),
        name="conv_ffn",
    )(x, g_pre.reshape(1, D), modg, w_up_bf16, w_up_bf16, conv_w, conv_w, cb, cb, w_down_bf16,
      g_post.reshape(1, D))


def _w_in_layout_kernel(w_ref, o_ref):
    o = np.cumsum([0, Q_LORA, KV_LORA, MLA_ROPE, FNET_WIDTH, 3 * RWKV_WIDTH, W_LORA, A_LORA, G_LORA])
    w = w_ref[0]
    q_dn, kv_dn, k_rope, xf, rkv, w_lo, a_lo, g_lo = [w[:, o[i]:o[i + 1]] for i in range(8)]
    zeros = jnp.zeros((w.shape[0], LANE - MLA_ROPE), w.dtype)
    tail = jnp.zeros((w.shape[0], IN_PAD - COL_GLO - G_LORA), w.dtype)
    o_ref[0] = jnp.concatenate([rkv, q_dn, xf, kv_dn, k_rope, zeros, w_lo, a_lo, g_lo, tail],
                               axis=-1).astype(o_ref.dtype)


def _layout_w_in(w):
    L, D, N = w.shape
    tk = 256
    return pl.pallas_call(
        _w_in_layout_kernel,
        grid=(L, D // tk),
        in_specs=[pl.BlockSpec((1, tk, N), lambda l, i: (l, i, 0))],
        out_specs=pl.BlockSpec((1, tk, IN_PAD), lambda l, i: (l, i, 0)),
        out_shape=jax.ShapeDtypeStruct((L, D, IN_PAD), BF16),
        compiler_params=_params(("parallel", "parallel"), 8 * tk * IN_PAD * 4),
        name="w_in_layout",
    )(w)


def _layout_w_uq(w):
    L = w.shape[0]
    w = w.reshape(L, Q_LORA, MLA_HEADS, MLA_NOPE + MLA_ROPE)
    w = jnp.pad(w, ((0, 0), (0, 0), (0, 0), (0, HEAD_PAD - MLA_NOPE - MLA_ROPE)))
    return w.reshape(L, Q_LORA, MLA_HEADS * HEAD_PAD).astype(BF16)


def _layout_w_ukv(w):
    L = w.shape[0]
    w = w.reshape(L, KV_LORA, MLA_HEADS, MLA_NOPE + MLA_V)
    wk = w[..., :MLA_NOPE].reshape(L, KV_LORA, MLA_HEADS * MLA_NOPE)
    wv = w[..., MLA_NOPE:].reshape(L, KV_LORA, MLA_HEADS * MLA_V)
    return wk.astype(BF16), wv.astype(BF16)


def _rope_table(tm):
    t = jnp.arange(DEC_SEQ)
    rowp = (t // GRID_W).astype(F32)
    colp = (t % GRID_W).astype(F32)
    n = MLA_ROPE // 4
    inv = ROPE_BASE ** (-jnp.arange(n, dtype=F32) / n)
    ang = jnp.concatenate([rowp[:, None] * inv, colp[:, None] * inv], axis=-1)
    cos = jnp.concatenate([jnp.ones((tm, MLA_ROPE // 2), F32), jnp.cos(ang)], axis=0)
    sin = jnp.concatenate([jnp.zeros((tm, MLA_ROPE // 2), F32), jnp.sin(ang)], axis=0)
    z32 = jnp.zeros_like(cos)
    z64 = jnp.zeros((cos.shape[0], LANE - MLA_ROPE), F32)
    mul = jnp.concatenate([cos, cos, z64], axis=1)
    left = jnp.concatenate([-sin, z32, z64], axis=1)
    right = jnp.concatenate([z32, sin, z64], axis=1)
    return jnp.stack([mul, left, right])


def kernel(x_prompt, x_sample, cache_mla_ckv, cache_mla_krope, state_rwkv, c, c_ctx, w_mod, b_mod, g_pre_mix, g_post_mix, g_pre_ffn, g_post_ffn, w_in, g_q_norm, w_uq, g_kv_norm, w_ukv, rwkv_conv, rwkv_w0, rwkv_w2, rwkv_a0, rwkv_a2, rwkv_g2, rwkv_k_k, rwkv_k_a, rwkv_r_k, rwkv_gn_g, rwkv_gn_b, w_out, ffn_w_up, ffn_conv, ffn_conv_b, ffn_w_down):
    Bc, Tc, D = x_prompt.shape
    Bl, Tl, _ = x_sample.shape
    L = w_mod.shape[0]
    assert (Tc, Tl, D) == (SEQ, DEC_SEQ, D_MODEL) and cache_mla_ckv.shape[2] == PAST_LEN
    assert Bl + 1 <= 8 and (Bc * Tc) % DEC_SEQ == 0
    n_ctx = Bc * Tc
    n_lat = Bl * Tl
    x = jnp.concatenate([x_prompt.reshape(n_ctx, D), x_sample.reshape(n_lat, D)], axis=0)

    cond = jnp.zeros((8, D), F32).at[0].set(c_ctx).at[1:1 + Bl].set(c)
    mods = _modulation(cond, w_mod, b_mod)
    group_row = np.concatenate([np.zeros(n_ctx // ROW_GROUP, np.int32),
                                1 + np.arange(n_lat // ROW_GROUP, dtype=np.int32) // (Tl // ROW_GROUP)])

    tm_mla = 512
    rope_tab = _rope_table(tm_mla)
    ridx = _rope_index(tm_mla, n_ctx)
    ridx_cache = lambda i: 0
    zero_state = jnp.zeros((Bc, 2, RWKV_HEADS, RWKV_HEAD_DIM, RWKV_HEAD_DIM), F32)

    w_in_b = _layout_w_in(w_in)
    w_uq_b = _layout_w_uq(w_uq)
    wk_b, wv_b = _layout_w_ukv(w_ukv)
    w_out_b = w_out.astype(BF16)
    w_up_b = ffn_w_up.astype(BF16)
    w_down_b = ffn_w_down.astype(BF16)

    ckv_out, krope_out, state_out = [], [], []
    for l in range(L):
        modg = mods[l][group_row].reshape(-1, 6, D)
        p = dict(rwkv_conv=rwkv_conv[l], rwkv_w0=rwkv_w0[l], rwkv_w2=rwkv_w2[l], rwkv_a0=rwkv_a0[l],
                 rwkv_a2=rwkv_a2[l], rwkv_g2=rwkv_g2[l], rwkv_k_k=rwkv_k_k[l], rwkv_k_a=rwkv_k_a[l],
                 rwkv_r_k=rwkv_r_k[l])

        hin = _in_projection(x, g_pre_mix[l], modg, w_in_b, l)

        q = _q_projection(hin, g_q_norm[l], w_uq_b, l, rope_tab, n_ctx)
        ckv, k, v = _kv_projection(hin, COL_KVDN // KV_LORA, COL_KROPE // LANE, g_kv_norm[l], wk_b, wv_b, l,
                                   rope_tab, ridx, tm_mla, True)
        cache_src = jnp.concatenate(
            [cache_mla_ckv[:, l].reshape(Bl * PAST_LEN, KV_LORA),
             jnp.pad(cache_mla_krope[:, l].reshape(Bl * PAST_LEN, MLA_ROPE), ((0, 0), (0, LANE - MLA_ROPE)))],
            axis=1)
        k_c, v_c = _kv_projection(cache_src, 0, KV_LORA // LANE, g_kv_norm[l], wk_b, wv_b, l,
                                  rope_tab, ridx_cache, PAST_LEN, False)
        attn = (_attention(q, 0, Bc, Tc, [(k, v, 0, Tc)]),
                _attention(q, n_ctx, Bl, Tl, [(k_c, v_c, 0, PAST_LEN), (k, v, n_ctx, Tl)]))

        four = (_fourier_mix(hin, 0, Bc, Tc), _fourier_mix(hin, n_ctx, Bl, Tl))

        prep = _rwkv_prep(hin, p, n_ctx)
        yf_ctx, yb_ctx, s_ctx = _rwkv_scan(prep, 0, Bc, Tc, zero_state)
        yf_lat, yb_lat, _ = _rwkv_scan(prep, n_ctx, Bl, Tl, state_rwkv[:, l])
        rw = _rwkv_finalize(list(yf_ctx) + list(yf_lat), list(yb_ctx) + list(yb_lat), prep[4], prep[3],
                            rwkv_gn_g[l], rwkv_gn_b[l])

        x = _out_projection(attn, four, rw, w_out_b, l, x, g_post_mix[l], modg)
        ffn = functools.partial(_conv_ffn, x, g_pre_ffn[l], g_post_ffn[l], modg, w_up_b, w_down_b, l,
                                ffn_conv[l], ffn_conv_b[l], n_ctx)
        if l + 1 < L:
            x = ffn(0, n_ctx + n_lat)
        else:
            y_prompt, y_sample = ffn(0, n_ctx), ffn(n_ctx, n_lat)

        ckv_out.append(ckv[:n_ctx].reshape(Bc, Tc, KV_LORA))
        krope_out.append(hin[:n_ctx, COL_KROPE:COL_KROPE + MLA_ROPE].reshape(Bc, Tc, MLA_ROPE))
        state_out.append(s_ctx)

    return (y_prompt.reshape(Bc, Tc, D), y_sample.reshape(Bl, Tl, D),
            jnp.stack(ckv_out, axis=1), jnp.stack(krope_out, axis=1), jnp.stack(state_out, axis=1))
```

```python
import functools
import math

import jax
import jax.numpy as jnp
import numpy as np
from jax import lax
from jax.experimental import pallas as pl
from jax.experimental.pallas import tpu as pltpu

F32 = jnp.float32
BF16 = jnp.bfloat16

D_MODEL = 2048
SEQ = 256
DEC_SEQ = 1024
PAST_LEN = 256
GRID_W = 64
MLA_HEADS = 8
MLA_NOPE = 128
MLA_ROPE = 64
MLA_V = 128
Q_LORA = 512
KV_LORA = 256
FNET_GROUPS = 4
FNET_GROUP_DIM = 128
FNET_WIDTH = FNET_GROUPS * FNET_GROUP_DIM
RWKV_HEADS = 8
RWKV_HEAD_DIM = 64
RWKV_WIDTH = RWKV_HEADS * RWKV_HEAD_DIM
W_LORA = 64
A_LORA = 64
G_LORA = 128
D_FF = 5632
ROPE_BASE = 10000.0
EPS = 1e-6
GN_EPS = 64e-5
DECAY_SCALE = math.exp(-0.5)

LANE = 128
ROW_GROUP = 256
HEAD_PAD = 256
CHUNK = 64
VMEM_CAP = 60 * 1024 * 1024

COL_RKV = 0
COL_QDN = 3 * RWKV_WIDTH
COL_XF = COL_QDN + Q_LORA
COL_KVDN = COL_XF + FNET_WIDTH
COL_KROPE = COL_KVDN + KV_LORA
COL_WA = COL_KROPE + LANE
COL_GLO = COL_WA + LANE
MXU_COLS = 256
IN_PAD = -(-(COL_GLO + G_LORA) // MXU_COLS) * MXU_COLS


SMALL_OPERANDS = 8 * 1024 * 1024


def _params(sem, nbytes):
    return pltpu.CompilerParams(dimension_semantics=sem,
                                vmem_limit_bytes=int(min(VMEM_CAP, nbytes + SMALL_OPERANDS)))


def _dot(a, b):
    return jnp.dot(a, b, preferred_element_type=F32)


def _dot_nt(a, b):
    return lax.dot_general(a, b, (((1,), (1,)), ((), ())), preferred_element_type=F32)


def _dot_tn(a, b):
    return lax.dot_general(a, b, (((0,), (0,)), ((), ())), preferred_element_type=F32)


def _split_bf16(x, n):
    terms = []
    rem = x
    for i in range(n):
        t = rem.astype(BF16)
        terms.append(t)
        if i + 1 < n:
            rem = rem - t.astype(F32)
    return terms


def _dot_exact_rhs(a, b_exact_bf16, n):
    acc = None
    for t in _split_bf16(a, n):
        p = _dot(t, b_exact_bf16)
        acc = p if acc is None else acc + p
    return acc


def _rms(x, g):
    return x * lax.rsqrt(jnp.mean(x * x, axis=-1, keepdims=True) + EPS) * g


def _sigmoid(x):
    return 1.0 / (1.0 + jnp.exp(-x))


def _mod_kernel(cond_ref, w_ref, b_ref, o_ref):
    c = cond_ref[...]
    s = (c * _sigmoid(c)).astype(BF16)
    o_ref[0] = _dot(s, w_ref[0].astype(BF16)) + b_ref[0]


def _modulation(cond, w_mod, b_mod):
    L, D, N = w_mod.shape
    tn = 2048
    return pl.pallas_call(
        _mod_kernel,
        grid=(L, N // tn),
        in_specs=[pl.BlockSpec((8, D), lambda l, j: (0, 0)),
                  pl.BlockSpec((1, D, tn), lambda l, j: (l, 0, j)),
                  pl.BlockSpec((1, 1, tn), lambda l, j: (l, 0, j))],
        out_specs=pl.BlockSpec((1, 8, tn), lambda l, j: (l, 0, j)),
        out_shape=jax.ShapeDtypeStruct((L, 8, N), F32),
        compiler_params=_params(("parallel", "parallel"), 3 * D * tn * 4),
        name="modulation",
    )(cond, w_mod, b_mod.reshape(L, 1, N))


def _inproj_kernel(x_ref, g_ref, mod_ref, w_ref, o_ref):
    h = _rms(x_ref[...], g_ref[...]) * (1.0 + mod_ref[0, 1:2, :]) + mod_ref[0, 0:1, :]
    o_ref[...] = _dot(h.astype(BF16), w_ref[...])


def _in_projection(x, g, modg, w_bf16, l):
    R, D = x.shape
    N = w_bf16.shape[2]
    tm = 512
    return pl.pallas_call(
        _inproj_kernel,
        grid=(R // tm,),
        in_specs=[pl.BlockSpec((tm, D), lambda i: (i, 0)),
                  pl.BlockSpec((1, D), lambda i: (0, 0)),
                  pl.BlockSpec((1, 6, D), lambda i: (i * (tm // ROW_GROUP), 0, 0)),
                  pl.BlockSpec((None, D, N), lambda i: (l, 0, 0), pipeline_mode=pl.Buffered(1))],
        out_specs=pl.BlockSpec((tm, N), lambda i: (i, 0)),
        out_shape=jax.ShapeDtypeStruct((R, N), F32),
        compiler_params=_params(("parallel",), 2 * tm * D * 4 + tm * D * 2 + D * N * 2 + 3 * tm * N * 4),
        name="in_projection",
    )(x, g.reshape(1, D), modg, w_bf16)


def _rope_block(blk, rope_ref):
    return (blk * rope_ref[0]
            + pltpu.roll(blk, LANE - MLA_ROPE // 2, 1) * rope_ref[1]
            + pltpu.roll(blk, MLA_ROPE // 2, 1) * rope_ref[2])


def _q_kernel(qdn_ref, g_ref, w_ref, rope_ref, o_ref):
    scale = (MLA_NOPE + MLA_ROPE) ** -0.5
    qn = _rms(qdn_ref[...], g_ref[...]).astype(BF16)
    q = _dot(qn, w_ref[...]) * scale
    parts = []
    for h in range(MLA_HEADS):
        base = h * HEAD_PAD
        parts.append(q[:, base:base + MLA_NOPE])
        parts.append(_rope_block(q[:, base + MLA_NOPE:base + HEAD_PAD], rope_ref))
    o_ref[...] = jnp.concatenate(parts, axis=1).astype(o_ref.dtype)


def _rope_index(tm, n_ctx_rows):
    n_ctx_tiles = n_ctx_rows // tm
    per_seq = DEC_SEQ // tm

    def index(i):
        return jnp.where(i < n_ctx_tiles, 0, 1 + (i - n_ctx_tiles) % per_seq)
    return index


def _q_projection(hin, g, w_bf16, l, rope_tab, n_ctx_rows):
    R = hin.shape[0]
    tm = 512
    N = MLA_HEADS * HEAD_PAD
    ridx = _rope_index(tm, n_ctx_rows)
    return pl.pallas_call(
        _q_kernel,
        grid=(R // tm,),
        in_specs=[pl.BlockSpec((tm, Q_LORA), lambda i: (i, COL_QDN // Q_LORA)),
                  pl.BlockSpec((1, Q_LORA), lambda i: (0, 0)),
                  pl.BlockSpec((None, Q_LORA, N), lambda i: (l, 0, 0)),
                  pl.BlockSpec((3, tm, LANE), lambda i: (0, ridx(i), 0))],
        out_specs=pl.BlockSpec((tm, N), lambda i: (i, 0)),
        out_shape=jax.ShapeDtypeStruct((R, N), BF16),
        compiler_params=_params(("parallel",), 2 * Q_LORA * N * 2 + 6 * tm * N * 4),
        name="q_projection",
    )(hin, g.reshape(1, Q_LORA), w_bf16, rope_tab)


def _kv_kernel(kvdn_ref, kr_ref, g_ref, wk_ref, wv_ref, rope_ref, *out_refs, normalize):
    k_ref, v_ref = out_refs[-2:]
    c = kvdn_ref[...]
    if normalize:
        c = _rms(c, g_ref[...])
        out_refs[0][...] = c
    cb = c.astype(BF16)
    kn = _dot(cb, wk_ref[...])
    v_ref[...] = _dot(cb, wv_ref[...]).astype(v_ref.dtype)
    kr = _rope_block(kr_ref[...], rope_ref)
    parts = []
    for h in range(MLA_HEADS):
        parts.append(kn[:, h * MLA_NOPE:(h + 1) * MLA_NOPE])
        parts.append(kr)
    k_ref[...] = jnp.concatenate(parts, axis=1).astype(k_ref.dtype)


def _kv_projection(src, kv_col, kr_col, g, wk_bf16, wv_bf16, l, rope_tab, ridx, tm, normalize):
    R = src.shape[0]
    NK = MLA_HEADS * HEAD_PAD
    NV = MLA_HEADS * MLA_V
    return pl.pallas_call(
        functools.partial(_kv_kernel, normalize=normalize),
        grid=(R // tm,),
        in_specs=[pl.BlockSpec((tm, KV_LORA), lambda i: (i, kv_col)),
                  pl.BlockSpec((tm, LANE), lambda i: (i, kr_col)),
                  pl.BlockSpec((1, KV_LORA), lambda i: (0, 0)),
                  pl.BlockSpec((None, KV_LORA, MLA_HEADS * MLA_NOPE), lambda i: (l, 0, 0)),
                  pl.BlockSpec((None, KV_LORA, NV), lambda i: (l, 0, 0)),
                  pl.BlockSpec((3, tm, LANE), lambda i: (0, ridx(i), 0))],
        out_specs=([pl.BlockSpec((tm, KV_LORA), lambda i: (i, 0))] if normalize else [])
                  + [pl.BlockSpec((tm, NK), lambda i: (i, 0)), pl.BlockSpec((tm, NV), lambda i: (i, 0))],
        out_shape=([jax.ShapeDtypeStruct((R, KV_LORA), F32)] if normalize else [])
                  + [jax.ShapeDtypeStruct((R, NK), BF16), jax.ShapeDtypeStruct((R, NV), BF16)],
        compiler_params=_params(("parallel",), 8 * tm * NK * 4),
        name="kv_projection" if normalize else "kv_projection_cache",
    )(src, src, g.reshape(1, KV_LORA), wk_bf16, wv_bf16, rope_tab)


def _attn_kernel(q_ref, *refs):
    o_ref = refs[-1]
    kv = [(refs[i], refs[i + 1]) for i in range(0, len(refs) - 1, 2)]
    for h in range(MLA_HEADS):
        qk = slice(h * HEAD_PAD, (h + 1) * HEAD_PAD)
        vo = slice(h * MLA_V, (h + 1) * MLA_V)
        q = q_ref[:, qk]
        scores = [_dot_nt(q, k_ref[:, qk]) for k_ref, _ in kv]
        m = functools.reduce(jnp.maximum, [jnp.max(s, axis=-1, keepdims=True) for s in scores])
        probs = [jnp.exp(s - m) for s in scores]
        l = functools.reduce(jnp.add, [jnp.sum(p, axis=-1, keepdims=True) for p in probs])
        o = functools.reduce(jnp.add, [_dot(p.astype(BF16), v_ref[:, vo]) for p, (_, v_ref) in zip(probs, kv)])
        o_ref[:, vo] = (o / l).astype(o_ref.dtype)


def _attention(q, q_row0, n_batch, tq_total, sources):
    tq = min(512, tq_total)
    nq = tq_total // tq
    q0 = q_row0 // tq
    NQ = MLA_HEADS * HEAD_PAD
    NV = MLA_HEADS * MLA_V
    in_specs = [pl.BlockSpec((tq, NQ), lambda b, i: (q0 + b * nq + i, 0))]
    args = [q]
    tk_total = 0
    for k, v, row0, tk in sources:
        in_specs.append(pl.BlockSpec((tk, NQ), lambda b, i, o=row0 // tk: (o + b, 0)))
        in_specs.append(pl.BlockSpec((tk, NV), lambda b, i, o=row0 // tk: (o + b, 0)))
        args += [k, v]
        tk_total += tk
    return pl.pallas_call(
        _attn_kernel,
        grid=(n_batch, nq),
        in_specs=in_specs,
        out_specs=pl.BlockSpec((tq, NV), lambda b, i: (b * nq + i, 0)),
        out_shape=jax.ShapeDtypeStruct((n_batch * tq_total, NV), BF16),
        compiler_params=_params(("parallel", "arbitrary"),
                                2 * (tq + tk_total) * (NQ + NV) * 2 + 6 * tq * tk_total * 4),
        name="attention",
    )(*args)


def _fnet_kernel(z_ref, cs_ref, ct_ref, o_ref):
    yc, ys = [], []
    for g in range(FNET_GROUPS):
        zg = z_ref[:, g * FNET_GROUP_DIM:(g + 1) * FNET_GROUP_DIM].astype(BF16)
        y = _dot(zg, cs_ref[...])
        yc.append(y[:, :FNET_GROUP_DIM])
        ys.append(y[:, FNET_GROUP_DIM:])
    stacked = jnp.concatenate([jnp.concatenate(yc, axis=1), jnp.concatenate(ys, axis=1)], axis=0)
    o_ref[...] = _dot(ct_ref[...], stacked.astype(BF16)).astype(o_ref.dtype)


def _dft_tables(T):
    def cs(n):
        i = np.arange(n, dtype=np.int64)
        ang = ((i[:, None] * i[None, :]) % n).astype(np.float64) * (2.0 * math.pi / n)
        return np.cos(ang), np.sin(ang)
    cc, sc = cs(FNET_GROUP_DIM)
    ct, st = cs(T)
    norm = 1.0 / math.sqrt(T * FNET_GROUP_DIM)
    return (jnp.asarray(np.concatenate([cc, sc], axis=1), F32).astype(BF16),
            jnp.asarray(np.concatenate([ct, -st], axis=1) * norm, F32).astype(BF16))


def _fourier_mix(hin, row0, n_batch, T):
    cs_tab, ct_tab = _dft_tables(T)
    b0 = row0 // T
    return pl.pallas_call(
        _fnet_kernel,
        grid=(n_batch,),
        in_specs=[pl.BlockSpec((T, FNET_WIDTH), lambda b: (b0 + b, COL_XF // FNET_WIDTH)),
                  pl.BlockSpec((FNET_GROUP_DIM, 2 * FNET_GROUP_DIM), lambda b: (0, 0)),
                  pl.BlockSpec((T, 2 * T), lambda b: (0, 0))],
        out_specs=pl.BlockSpec((T, FNET_WIDTH), lambda b: (b, 0)),
        out_shape=jax.ShapeDtypeStruct((n_batch * T, FNET_WIDTH), BF16),
        compiler_params=_params(("parallel",), 2 * T * 2 * T * 2 + 10 * T * FNET_WIDTH * 4),
        name="fourier_mix",
    )(hin, cs_tab, ct_tab)


def _head_sum_matrix(scale):
    i = np.arange(RWKV_WIDTH) // RWKV_HEAD_DIM
    return jnp.asarray((i[:, None] == i[None, :]).astype(np.float32) * scale, BF16)


def _rwkv_prep_kernel(x_ref, xp_ref, xn_ref, wa_ref, gl_ref, cw_ref, w0_ref, w2_ref, a0_ref, a2_ref, g2_ref,
                      kk_ref, ka_ref, rk_ref, hs_ref,
                      r_ref, kh_ref, v_ref, g_ref, bonus_ref, lw_ref, kt_ref, beta_ref,
                      *, n_ctx_tiles, tiles_per_seq):
    i = pl.program_id(0)
    j = i - n_ctx_tiles
    first = jnp.logical_or(i < n_ctx_tiles, j % tiles_per_seq == 0)
    last = jnp.logical_or(i < n_ctx_tiles, j % tiles_per_seq == tiles_per_seq - 1)
    x = x_ref[...]
    tm = x.shape[0]
    row = lax.broadcasted_iota(jnp.int32, x.shape, 0)
    prev_row = jnp.where(first, 0.0, xp_ref[7:8, :])
    next_row = jnp.where(last, 0.0, xn_ref[0:1, :])
    x_prev = jnp.where(row == 0, prev_row, pltpu.roll(x, 1, 0))
    x_next = jnp.where(row == tm - 1, next_row, pltpu.roll(x, tm - 1, 0))
    xc = x_prev * cw_ref[0:1, :] + x * cw_ref[1:2, :] + x_next * cw_ref[2:3, :]
    R_ = RWKV_WIDTH
    r, k, v = xc[:, :R_], xc[:, R_:2 * R_], xc[:, 2 * R_:]
    wl = jnp.tanh(wa_ref[:, :W_LORA]).astype(BF16)
    al = wa_ref[:, W_LORA:W_LORA + A_LORA].astype(BF16)
    g_ref[...] = _dot(_sigmoid(gl_ref[...]).astype(BF16), g2_ref[...].astype(BF16))
    kappa = k * kk_ref[...]
    ss = _dot_exact_rhs(kappa * kappa, hs_ref[...], 3)
    kh = kappa * lax.rsqrt(ss + EPS)
    kt_sum = None
    for d in range(2):
        lw_ref[d] = -DECAY_SCALE * _sigmoid(w0_ref[d:d + 1, :] + _dot(wl, w2_ref[d].astype(BF16)))
        a = _sigmoid(a0_ref[d:d + 1, :] + _dot(al, a2_ref[d].astype(BF16)))
        kt = k * (1.0 + (a - 1.0) * ka_ref[...])
        kt_ref[d] = kt.astype(kt_ref.dtype)
        beta_ref[d] = (a * kh).astype(beta_ref.dtype)
        kt_sum = kt if kt_sum is None else kt_sum + kt
    r_ref[...] = r.astype(r_ref.dtype)
    kh_ref[...] = kh.astype(kh_ref.dtype)
    v_ref[...] = v.astype(v_ref.dtype)
    bonus_ref[...] = _dot_exact_rhs(r * kt_sum * rk_ref[...], hs_ref[...], 3) * v


def _rwkv_prep(hin, p, n_ctx_rows):
    R = hin.shape[0]
    tm = ROW_GROUP
    W3 = 3 * RWKV_WIDTH
    nh = tm // 8
    last_blk = R // 8 - 1
    row = lambda a: a.reshape(1, RWKV_WIDTH)
    full = lambda *s: pl.BlockSpec(s, lambda i: (0,) * len(s))
    out_rw = lambda dt: jax.ShapeDtypeStruct((R, RWKV_WIDTH), dt)
    out_rw2 = lambda dt: jax.ShapeDtypeStruct((2, R, RWKV_WIDTH), dt)
    spec_rw = pl.BlockSpec((tm, RWKV_WIDTH), lambda i: (i, 0))
    spec_rw2 = pl.BlockSpec((2, tm, RWKV_WIDTH), lambda i: (0, i, 0))
    return pl.pallas_call(
        functools.partial(_rwkv_prep_kernel, n_ctx_tiles=n_ctx_rows // tm, tiles_per_seq=DEC_SEQ // tm),
        grid=(R // tm,),
        in_specs=[pl.BlockSpec((tm, W3), lambda i: (i, COL_RKV // W3)),
                  pl.BlockSpec((8, W3), lambda i: (jnp.maximum(i * nh - 1, 0), 0)),
                  pl.BlockSpec((8, W3), lambda i: (jnp.minimum((i + 1) * nh, last_blk), 0)),
                  pl.BlockSpec((tm, LANE), lambda i: (i, COL_WA // LANE)),
                  pl.BlockSpec((tm, G_LORA), lambda i: (i, COL_GLO // G_LORA)),
                  full(3, W3), full(2, RWKV_WIDTH), full(2, W_LORA, RWKV_WIDTH),
                  full(2, RWKV_WIDTH), full(2, A_LORA, RWKV_WIDTH), full(G_LORA, RWKV_WIDTH),
                  full(1, RWKV_WIDTH), full(1, RWKV_WIDTH), full(1, RWKV_WIDTH),
                  full(RWKV_WIDTH, RWKV_WIDTH)],
        out_specs=[spec_rw] * 5 + [spec_rw2] * 3,
        out_shape=[out_rw(BF16)] * 3 + [out_rw(F32)] * 2 + [out_rw2(F32), out_rw2(BF16), out_rw2(BF16)],
        compiler_params=_params(("parallel",), 40 * tm * W3 * 4),
        name="rwkv_prep",
    )(hin, hin, hin, hin, hin, p['rwkv_conv'], p['rwkv_w0'], p['rwkv_w2'], p['rwkv_a0'], p['rwkv_a2'],
      p['rwkv_g2'], row(p['rwkv_k_k']), row(p['rwkv_k_a']), row(p['rwkv_r_k']), _head_sum_matrix(1.0))


def _rwkv_scan_kernel(*refs, n_seq):
    seq_refs = [refs[13 * p:13 * (p + 1)] for p in range(n_seq)]
    mi_ref = refs[13 * n_seq]
    out_refs = refs[13 * n_seq + 1:-1]
    s_scr = refs[-1]
    c = pl.program_id(1)
    nc = pl.num_programs(1)
    N = RWKV_HEAD_DIM
    C = CHUNK
    H = RWKV_HEADS

    @pl.when(c == 0)
    def _():
        for p in range(n_seq):
            s_scr[p] = seq_refs[p][12][0]

    eye = (lax.broadcasted_iota(jnp.int32, (C, C), 0) == lax.broadcasted_iota(jnp.int32, (C, C), 1)).astype(F32)
    dirs = [(p, d, tuple(sr[3 * d:3 * d + 3]) + tuple(sr[6 + 3 * d:9 + 3 * d]))
            for p, sr in enumerate(seq_refs) for d in range(2)]
    chains = []
    for p, d, (r_ref, kh_ref, v_ref, lw_ref, kt_ref, beta_ref) in dirs:
        mi = mi_ref[d]
        lw = lw_ref[0]
        cl = _dot_exact_lhs(mi.astype(BF16), lw, 3)
        cl_tot = jnp.sum(lw, axis=0, keepdims=True)
        g_inv = jnp.exp(-cl)
        g_end = jnp.exp(cl_tot - cl)
        g_tot = jnp.exp(cl_tot)
        kh, beta, kt = kh_ref[...], beta_ref[0], kt_ref[0]
        k_dec = kh * jnp.exp(cl - lw)
        r_dec = r_ref[...] * jnp.exp(cl)
        b_inv, k_inv = beta * g_inv, kt * g_inv
        k_end, b_end = kt * g_end, beta * g_end
        v_all = v_ref[...]
        for h in range(H):
            sl = slice(h * N, (h + 1) * N)
            chains.append(dict(p=p, d=d, h=h, mi=mi, ms=mi - eye, kd=k_dec[:, sl], rd=r_dec[:, sl],
                               bi=b_inv[:, sl], ki=k_inv[:, sl], ke=k_end[:, sl], be=b_end[:, sl],
                               v=v_all[:, sl], gt=g_tot[:, sl], s0=s_scr[p, d, h]))

    for ch in chains:
        lhs = jnp.concatenate([ch['kd'], ch['rd']], axis=0).astype(BF16)
        rhs = jnp.concatenate([ch['bi'], ch['ki']], axis=0).astype(BF16)
        nt = _dot_nt(lhs, rhs)
        ch['a'] = nt[:C, :C] * ch['ms']
        ch['b'] = (nt[:C, C:] * ch['ms']).astype(BF16)
        ch['qp'] = (nt[C:, :] * jnp.concatenate([-ch['mi'], ch['mi']], axis=1)).astype(BF16)
        ch['lhs'] = lhs
        ch['t'] = eye - ch['a']
    for ch in chains:
        xb = (-ch['a']).astype(BF16)
        ch['xp'] = _dot(xb, xb).astype(BF16)
    n_sq = int(math.log2(C)) - 1
    for k in range(n_sq):
        last = k == n_sq - 1
        for ch in chains:
            tb = ch['t'].astype(BF16)
            prod = _dot(tb if last else jnp.concatenate([tb, ch['xp']], axis=0), ch['xp'])
            ch['t'] = ch['t'] + prod[:C]
            if not last:
                ch['xp'] = prod[C:].astype(BF16)
    for ch in chains:
        ch['s0b'] = ch['s0'].astype(BF16)
        ch['vb'] = ch['v'].astype(BF16)
        ks = _dot_nt(ch['lhs'], ch['s0b'])
        ch['rs'] = ks[C:]
        ch['z'] = ks[:C] + _dot(ch['b'], ch['vb'])
    for ch in chains:
        u = _dot(ch['t'].astype(BF16), ch['z'].astype(BF16))
        ch['uv'] = jnp.concatenate([u.astype(BF16), ch['vb']], axis=0)
    for ch in chains:
        kb = jnp.concatenate([-ch['be'], ch['ke']], axis=0).astype(BF16)
        s_scr[ch['p'], ch['d'], ch['h']] = ch['s0'] * ch['gt'] + _dot_tn(ch['uv'], kb)
    for ch in chains:
        ch['y'] = ch['rs'] + _dot(ch['qp'], ch['uv'])
    for p in range(n_seq):
        for d in range(2):
            ys = [ch['y'] for ch in chains[(2 * p + d) * H:(2 * p + d + 1) * H]]
            out_refs[3 * p + d][...] = jnp.concatenate(ys, axis=1)

    @pl.when(c == nc - 1)
    def _():
        for p in range(n_seq):
            out_refs[3 * p + 2][0] = s_scr[p]


def _dot_exact_lhs(a_exact_bf16, b, n):
    acc = None
    for t in _split_bf16(b, n):
        p = _dot(a_exact_bf16, t)
        acc = p if acc is None else acc + p
    return acc


def _causal_masks():
    t = np.arange(CHUNK)
    fwd = (t[None, :] <= t[:, None]).astype(np.float32)
    return jnp.asarray(np.stack([fwd, fwd.T]))


def _rwkv_scan(prep, row0, n_batch, T, s0):
    r, kh, v, _, _, lw, kt, beta = prep
    C = CHUNK
    nc = T // C
    c0 = row0 // C
    W = RWKV_WIDTH
    n_seq = 2 if n_batch % 2 == 0 else 1
    nb = n_batch // n_seq
    st_shape = (1, 2, RWKV_HEADS, RWKV_HEAD_DIM, RWKV_HEAD_DIM)
    in_specs, args, out_specs, out_shape = [], [], [], []
    for p in range(n_seq):
        fwd = lambda b, c, p=p: (p * nb + b) * nc + c
        bwd = lambda b, c, p=p: (p * nb + b) * nc + nc - 1 - c
        shared_f = pl.BlockSpec((C, W), lambda b, c, f=fwd: (c0 + f(b, c), 0))
        shared_b = pl.BlockSpec((C, W), lambda b, c, f=bwd: (c0 + f(b, c), 0))
        dir_f = pl.BlockSpec((1, C, W), lambda b, c, f=fwd: (0, c0 + f(b, c), 0))
        dir_b = pl.BlockSpec((1, C, W), lambda b, c, f=bwd: (1, c0 + f(b, c), 0))
        in_specs += [shared_f] * 3 + [shared_b] * 3 + [dir_f] * 3 + [dir_b] * 3
        in_specs.append(pl.BlockSpec(st_shape, lambda b, c, p=p: (p * nb + b, 0, 0, 0, 0)))
        args += [r, kh, v, r, kh, v, lw, kt, beta, lw, kt, beta, s0]
        out_specs += [pl.BlockSpec((C, W), lambda b, c: (b * nc + c, 0)),
                      pl.BlockSpec((C, W), lambda b, c: (b * nc + nc - 1 - c, 0)),
                      pl.BlockSpec(st_shape, lambda b, c: (b, 0, 0, 0, 0))]
        out_shape += [jax.ShapeDtypeStruct((nb * T, W), F32)] * 2
        out_shape.append(jax.ShapeDtypeStruct((nb,) + st_shape[1:], F32))
    in_specs.append(pl.BlockSpec((2, C, C), lambda b, c: (0, 0, 0)))
    outs = pl.pallas_call(
        functools.partial(_rwkv_scan_kernel, n_seq=n_seq),
        grid=(nb, nc),
        in_specs=in_specs,
        out_specs=out_specs,
        out_shape=out_shape,
        scratch_shapes=[pltpu.VMEM((n_seq, 2, RWKV_HEADS, RWKV_HEAD_DIM, RWKV_HEAD_DIM), F32)],
        compiler_params=_params(("parallel", "arbitrary"), 32 * 1024 * 1024),
        name="rwkv_scan",
    )(*args, _causal_masks())
    return outs[0::3], outs[1::3], jnp.concatenate(outs[2::3], axis=0)


def _part_tiles(parts, tm):
    return [int(t) for t in np.cumsum([0] + [p.shape[0] // tm for p in parts])]


def _part_specs(parts, tm):
    starts = _part_tiles(parts, tm)
    return [pl.BlockSpec((tm, p.shape[1]), lambda i, t0=t0, n=t1 - t0: (jnp.clip(i - t0, 0, n - 1), 0))
            for p, t0, t1 in zip(parts, starts[:-1], starts[1:])]


def _for_own_part(starts, run):
    i = pl.program_id(0)
    for k, (t0, t1) in enumerate(zip(starts[:-1], starts[1:])):
        pl.when(jnp.logical_and(i >= t0, i < t1))(functools.partial(run, k))


def _rwkv_fin_kernel(*refs, starts):
    n = len(starts) - 1
    yf_refs, yb_refs = refs[:n], refs[n:2 * n]
    bonus_ref, g_ref, gg_ref, gb_ref, hm_ref, o_ref = refs[2 * n:]

    def run(k):
        y = yf_refs[k][...] + yb_refs[k][...]
        mu = _dot_exact_rhs(y, hm_ref[...], 3)
        yc = y - mu
        var = _dot_exact_rhs(yc * yc, hm_ref[...], 3)
        yn = yc * lax.rsqrt(var + GN_EPS) * gg_ref[...] + gb_ref[...]
        o_ref[...] = ((yn + bonus_ref[...]) * g_ref[...]).astype(o_ref.dtype)

    _for_own_part(starts, run)


def _rwkv_finalize(yf_parts, yb_parts, bonus, g, gn_g, gn_b):
    tm = 512
    W = RWKV_WIDTH
    starts = _part_tiles(yf_parts, tm)
    rows = pl.BlockSpec((tm, W), lambda i: (i, 0))
    vec = pl.BlockSpec((1, W), lambda i: (0, 0))
    return pl.pallas_call(
        functools.partial(_rwkv_fin_kernel, starts=starts),
        grid=(starts[-1],),
        in_specs=_part_specs(yf_parts, tm) + _part_specs(yb_parts, tm)
                 + [rows, rows, vec, vec, pl.BlockSpec((W, W), lambda i: (0, 0))],
        out_specs=rows,
        out_shape=jax.ShapeDtypeStruct((starts[-1] * tm, W), BF16),
        compiler_params=_params(("parallel",), 24 * tm * W * 4),
        name="rwkv_finalize",
    )(*yf_parts, *yb_parts, bonus, g, gn_g.reshape(1, W), gn_b.reshape(1, W),
      _head_sum_matrix(1.0 / RWKV_HEAD_DIM))


def _outproj_kernel(ac_ref, al_ref, fc_ref, fl_ref, rw_ref, w_ref, x_ref, g_ref, mod_ref, o_ref, *, starts):
    a0 = MLA_HEADS * MLA_V
    a1 = a0 + FNET_WIDTH

    def run(k):
        attn_ref, four_ref = ((ac_ref, fc_ref), (al_ref, fl_ref))[k]
        acc = _dot(attn_ref[...], w_ref[0:a0, :])
        acc += _dot(four_ref[...], w_ref[a0:a1, :])
        acc += _dot(rw_ref[...], w_ref[a1:, :])
        o_ref[...] = x_ref[...] + mod_ref[0, 2:3, :] * _rms(acc, g_ref[...])

    _for_own_part(starts, run)


def _out_projection(attn_parts, four_parts, rw, w_bf16, l, x, g, modg):
    R, D = x.shape
    tm = 512
    K = w_bf16.shape[1]
    return pl.pallas_call(
        functools.partial(_outproj_kernel, starts=_part_tiles(attn_parts, tm)),
        grid=(R // tm,),
        in_specs=_part_specs(attn_parts, tm) + _part_specs(four_parts, tm)
                 + [pl.BlockSpec((tm, rw.shape[1]), lambda i: (i, 0)),
                  pl.BlockSpec((None, K, D), lambda i: (l, 0, 0)),
                  pl.BlockSpec((tm, D), lambda i: (i, 0)),
                  pl.BlockSpec((1, D), lambda i: (0, 0)),
                  pl.BlockSpec((1, 6, D), lambda i: (i * (tm // ROW_GROUP), 0, 0))],
        out_specs=pl.BlockSpec((tm, D), lambda i: (i, 0)),
        out_shape=jax.ShapeDtypeStruct((R, D), F32),
        compiler_params=_params(("parallel",), 2 * K * D * 2 + 8 * tm * D * 4),
        name="out_projection",
    )(attn_parts[0], attn_parts[1], four_parts[0], four_parts[1], rw, w_bf16, x, g.reshape(1, D), modg)


FFN_SUB = 256
FFN_ACC_COLS = 512


def _ffn_kernel(x_ref, gpre_ref, mod_ref, wg_ref, wv_ref, cwg_ref, cwv_ref, cbg_ref, cbv_ref, wd_ref, gpost_ref,
                o_ref, a_scr, *, is_ctx_tile):
    j = pl.program_id(1)
    acc_scr = o_ref

    @pl.when(j == 0)
    def _():
        h = _rms(x_ref[...], gpre_ref[...]) * (1.0 + mod_ref[0, 4:5, :]) + mod_ref[0, 3:4, :]
        a_scr[...] = h.astype(BF16)
        acc_scr[...] = jnp.zeros_like(acc_scr)

    tm, D = acc_scr.shape
    tn = wg_ref.shape[1]
    seq = jnp.where(is_ctx_tile(pl.program_id(0)), SEQ, DEC_SEQ)
    pos = lax.broadcasted_iota(jnp.int32, (tm, LANE), 0) & (seq - 1)
    keep_prev = (pos != 0).astype(F32)
    keep_next = (pos != seq - 1).astype(F32)
    a = a_scr[...]

    def conv(u, cw_ref, cb_ref, cs):
        u_prev = pltpu.roll(u, 1, 0) * keep_prev
        u_next = pltpu.roll(u, tm - 1, 0) * keep_next
        return u_prev * cw_ref[0:1, cs] + u * cw_ref[1:2, cs] + u_next * cw_ref[2:3, cs] + cb_ref[:, cs]

    subs = [slice(s * FFN_SUB, (s + 1) * FFN_SUB) for s in range(tn // FFN_SUB)]
    ups = [(_dot(a, wg_ref[:, cs]), _dot(a, wv_ref[:, cs])) for cs in subs]
    for cs, (ug, uv) in zip(subs, ups):
        halves = []
        for h0 in range(0, FFN_SUB, LANE):
            hs = slice(h0, h0 + LANE)
            ch = slice(cs.start + h0, cs.start + h0 + LANE)
            gate = conv(ug[:, hs], cwg_ref, cbg_ref, ch)
            val = conv(uv[:, hs], cwv_ref, cbv_ref, ch)
            halves.append((gate * _sigmoid(gate) * val).astype(BF16))
        act = jnp.concatenate(halves, axis=1)
        for n in range(D // FFN_ACC_COLS):
            ns = slice(n * FFN_ACC_COLS, (n + 1) * FFN_ACC_COLS)
            acc_scr[:, ns] += _dot(act, wd_ref[cs, ns])

    @pl.when(j == pl.num_programs(1) - 1)
    def _():
        o_ref[...] = x_ref[...] + mod_ref[0, 5:6, :] * _rms(acc_scr[...], gpost_ref[...])


def _conv_ffn(x, g_pre, g_post, modg, w_up_bf16, w_down_bf16, l, conv_w, conv_b, n_ctx_rows, row0, n_rows):
    D = x.shape[1]
    tm, tn = DEC_SEQ, 2 * FFN_SUB
    nj = D_FF // tn
    t0 = row0 // tm
    n_ctx_tiles = n_ctx_rows // tm
    cb = conv_b.reshape(1, 2 * D_FF)
    return pl.pallas_call(
        functools.partial(_ffn_kernel, is_ctx_tile=lambda i: i + t0 < n_ctx_tiles),
        grid=(n_rows // tm, nj),
        in_specs=[pl.BlockSpec((tm, D), lambda i, j: (t0 + i, 0)),
                  pl.BlockSpec((1, D), lambda i, j: (0, 0)),
                  pl.BlockSpec((1, 6, D), lambda i, j: ((t0 + i) * (tm // ROW_GROUP), 0, 0)),
                  pl.BlockSpec((None, D, tn), lambda i, j: (l, 0, j)),
                  pl.BlockSpec((None, D, tn), lambda i, j: (l, 0, j + nj)),
                  pl.BlockSpec((3, tn), lambda i, j: (0, j)),
                  pl.BlockSpec((3, tn), lambda i, j: (0, j + nj)),
                  pl.BlockSpec((1, tn), lambda i, j: (0, j)),
                  pl.BlockSpec((1, tn), lambda i, j: (0, j + nj)),
                  pl.BlockSpec((None, tn, D), lambda i, j: (l, j, 0)),
                  pl.BlockSpec((1, D), lambda i, j: (0, 0))],
        out_specs=pl.BlockSpec((tm, D), lambda i, j: (i, 0), pipeline_mode=pl.Buffered(1)),
        out_shape=jax.ShapeDtypeStruct((n_rows, D), F32),
        scratch_shapes=[pltpu.VMEM((tm, D), BF16)],
        compiler_params=_params(("parallel", "arbitrary"),
                                3 * tm * D * 4 + tm * D * 2 + 6 * D * tn * 2 + 16 * tm * FFN_SUB * 4),
        name="conv_ffn",
    )(x, g_pre.reshape(1, D), modg, w_up_bf16, w_up_bf16, conv_w, conv_w, cb, cb, w_down_bf16,
      g_post.reshape(1, D))


def _w_in_layout_kernel(w_ref, o_ref):
    o = np.cumsum([0, Q_LORA, KV_LORA, MLA_ROPE, FNET_WIDTH, 3 * RWKV_WIDTH, W_LORA, A_LORA, G_LORA])
    w = w_ref[0]
    q_dn, kv_dn, k_rope, xf, rkv, w_lo, a_lo, g_lo = [w[:, o[i]:o[i + 1]] for i in range(8)]
    zeros = jnp.zeros((w.shape[0], LANE - MLA_ROPE), w.dtype)
    tail = jnp.zeros((w.shape[0], IN_PAD - COL_GLO - G_LORA), w.dtype)
    o_ref[0] = jnp.concatenate([rkv, q_dn, xf, kv_dn, k_rope, zeros, w_lo, a_lo, g_lo, tail],
                               axis=-1).astype(o_ref.dtype)


def _layout_w_in(w):
    L, D, N = w.shape
    tk = 256
    return pl.pallas_call(
        _w_in_layout_kernel,
        grid=(L, D // tk),
        in_specs=[pl.BlockSpec((1, tk, N), lambda l, i: (l, i, 0))],
        out_specs=pl.BlockSpec((1, tk, IN_PAD), lambda l, i: (l, i, 0)),
        out_shape=jax.ShapeDtypeStruct((L, D, IN_PAD), BF16),
        compiler_params=_params(("parallel", "parallel"), 8 * tk * IN_PAD * 4),
        name="w_in_layout",
    )(w)


def _layout_w_uq(w):
    L = w.shape[0]
    w = w.reshape(L, Q_LORA, MLA_HEADS, MLA_NOPE + MLA_ROPE)
    w = jnp.pad(w, ((0, 0), (0, 0), (0, 0), (0, HEAD_PAD - MLA_NOPE - MLA_ROPE)))
    return w.reshape(L, Q_LORA, MLA_HEADS * HEAD_PAD).astype(BF16)


def _layout_w_ukv(w):
    L = w.shape[0]
    w = w.reshape(L, KV_LORA, MLA_HEADS, MLA_NOPE + MLA_V)
    wk = w[..., :MLA_NOPE].reshape(L, KV_LORA, MLA_HEADS * MLA_NOPE)
    wv = w[..., MLA_NOPE:].reshape(L, KV_LORA, MLA_HEADS * MLA_V)
    return wk.astype(BF16), wv.astype(BF16)


def _rope_table(tm):
    t = jnp.arange(DEC_SEQ)
    rowp = (t // GRID_W).astype(F32)
    colp = (t % GRID_W).astype(F32)
    n = MLA_ROPE // 4
    inv = ROPE_BASE ** (-jnp.arange(n, dtype=F32) / n)
    ang = jnp.concatenate([rowp[:, None] * inv, colp[:, None] * inv], axis=-1)
    cos = jnp.concatenate([jnp.ones((tm, MLA_ROPE // 2), F32), jnp.cos(ang)], axis=0)
    sin = jnp.concatenate([jnp.zeros((tm, MLA_ROPE // 2), F32), jnp.sin(ang)], axis=0)
    z32 = jnp.zeros_like(cos)
    z64 = jnp.zeros((cos.shape[0], LANE - MLA_ROPE), F32)
    mul = jnp.concatenate([cos, cos, z64], axis=1)
    left = jnp.concatenate([-sin, z32, z64], axis=1)
    right = jnp.concatenate([z32, sin, z64], axis=1)
    return jnp.stack([mul, left, right])


def kernel(x_prompt, x_sample, cache_mla_ckv, cache_mla_krope, state_rwkv, c, c_ctx, w_mod, b_mod, g_pre_mix, g_post_mix, g_pre_ffn, g_post_ffn, w_in, g_q_norm, w_uq, g_kv_norm, w_ukv, rwkv_conv, rwkv_w0, rwkv_w2, rwkv_a0, rwkv_a2, rwkv_g2, rwkv_k_k, rwkv_k_a, rwkv_r_k, rwkv_gn_g, rwkv_gn_b, w_out, ffn_w_up, ffn_conv, ffn_conv_b, ffn_w_down):
    Bc, Tc, D = x_prompt.shape
    Bl, Tl, _ = x_sample.shape
    L = w_mod.shape[0]
    assert (Tc, Tl, D) == (SEQ, DEC_SEQ, D_MODEL) and cache_mla_ckv.shape[2] == PAST_LEN
    assert Bl + 1 <= 8 and (Bc * Tc) % DEC_SEQ == 0
    n_ctx = Bc * Tc
    n_lat = Bl * Tl
    x = jnp.concatenate([x_prompt.reshape(n_ctx, D), x_sample.reshape(n_lat, D)], axis=0)

    cond = jnp.zeros((8, D), F32).at[0].set(c_ctx).at[1:1 + Bl].set(c)
    mods = _modulation(cond, w_mod, b_mod)
    group_row = np.concatenate([np.zeros(n_ctx // ROW_GROUP, np.int32),
                                1 + np.arange(n_lat // ROW_GROUP, dtype=np.int32) // (Tl // ROW_GROUP)])

    tm_mla = 512
    rope_tab = _rope_table(tm_mla)
    ridx = _rope_index(tm_mla, n_ctx)
    ridx_cache = lambda i: 0
    zero_state = jnp.zeros((Bc, 2, RWKV_HEADS, RWKV_HEAD_DIM, RWKV_HEAD_DIM), F32)

    w_in_b = _layout_w_in(w_in)
    w_uq_b = _layout_w_uq(w_uq)
    wk_b, wv_b = _layout_w_ukv(w_ukv)
    w_out_b = w_out.astype(BF16)
    w_up_b = ffn_w_up.astype(BF16)
    w_down_b = ffn_w_down.astype(BF16)

    ckv_out, krope_out, state_out = [], [], []
    for l in range(L):
        modg = mods[l][group_row].reshape(-1, 6, D)
        p = dict(rwkv_conv=rwkv_conv[l], rwkv_w0=rwkv_w0[l], rwkv_w2=rwkv_w2[l], rwkv_a0=rwkv_a0[l],
                 rwkv_a2=rwkv_a2[l], rwkv_g2=rwkv_g2[l], rwkv_k_k=rwkv_k_k[l], rwkv_k_a=rwkv_k_a[l],
                 rwkv_r_k=rwkv_r_k[l])

        hin = _in_projection(x, g_pre_mix[l], modg, w_in_b, l)

        q = _q_projection(hin, g_q_norm[l], w_uq_b, l, rope_tab, n_ctx)
        ckv, k, v = _kv_projection(hin, COL_KVDN // KV_LORA, COL_KROPE // LANE, g_kv_norm[l], wk_b, wv_b, l,
                                   rope_tab, ridx, tm_mla, True)
        cache_src = jnp.concatenate(
            [cache_mla_ckv[:, l].reshape(Bl * PAST_LEN, KV_LORA),
             jnp.pad(cache_mla_krope[:, l].reshape(Bl * PAST_LEN, MLA_ROPE), ((0, 0), (0, LANE - MLA_ROPE)))],
            axis=1)
        k_c, v_c = _kv_projection(cache_src, 0, KV_LORA // LANE, g_kv_norm[l], wk_b, wv_b, l,
                                  rope_tab, ridx_cache, PAST_LEN, False)
        attn = (_attention(q, 0, Bc, Tc, [(k, v, 0, Tc)]),
                _attention(q, n_ctx, Bl, Tl, [(k_c, v_c, 0, PAST_LEN), (k, v, n_ctx, Tl)]))

        four = (_fourier_mix(hin, 0, Bc, Tc), _fourier_mix(hin, n_ctx, Bl, Tl))

        prep = _rwkv_prep(hin, p, n_ctx)
        yf_ctx, yb_ctx, s_ctx = _rwkv_scan(prep, 0, Bc, Tc, zero_state)
        yf_lat, yb_lat, _ = _rwkv_scan(prep, n_ctx, Bl, Tl, state_rwkv[:, l])
        rw = _rwkv_finalize(list(yf_ctx) + list(yf_lat), list(yb_ctx) + list(yb_lat), prep[4], prep[3],
                            rwkv_gn_g[l], rwkv_gn_b[l])

        x = _out_projection(attn, four, rw, w_out_b, l, x, g_post_mix[l], modg)
        ffn = functools.partial(_conv_ffn, x, g_pre_ffn[l], g_post_ffn[l], modg, w_up_b, w_down_b, l,
                                ffn_conv[l], ffn_conv_b[l], n_ctx)
        if l + 1 < L:
            x = ffn(0, n_ctx + n_lat)
        else:
            y_prompt, y_sample = ffn(0, n_ctx), ffn(n_ctx, n_lat)

        ckv_out.append(ckv[:n_ctx].reshape(Bc, Tc, KV_LORA))
        krope_out.append(hin[:n_ctx, COL_KROPE:COL_KROPE + MLA_ROPE].reshape(Bc, Tc, MLA_ROPE))
        state_out.append(s_ctx)

    return (y_prompt.reshape(Bc, Tc, D), y_sample.reshape(Bl, Tl, D),
            jnp.stack(ckv_out, axis=1), jnp.stack(krope_out, axis=1), jnp.stack(state_out, axis=1))
```
